```python
import jax, jax.numpy as jnp
from jax import lax
import numpy as np

D_MODEL = 4096
BATCH = 2
SEQ = 8192
DEPTH = 1

N_MEM = 256
MEM_HEADS = 4
MEM_W = D_MODEL // 4
MEM_HD = MEM_W // MEM_HEADS
CONV_CH = D_MODEL // 4
CONV_WIDTH = 31
GLA_HEADS = 4
GLA_VAL = D_MODEL // 2
GLA_DV = GLA_VAL // GLA_HEADS
GLA_DK = GLA_DV // 2
GLA_KEY = GLA_HEADS * GLA_DK
GATE_RANK = 16
GATE_TAU = 16.0
CHUNK = 64
D_MIX = CONV_CH + GLA_VAL + MEM_W
SPLITS = (CONV_CH, CONV_CH, CONV_CH,
          GLA_KEY, GLA_KEY, GLA_VAL, GATE_RANK, GLA_VAL,
          MEM_W, MEM_W)
D_IN = sum(SPLITS)
LN_EPS = 1e-5
DN_ALPHA = (2 * DEPTH) ** 0.25
DN_BETA = (8 * DEPTH) ** -0.25

kernel_name = 'hybrid_conv_gla_memory_deepnorm'


def _layernorm(x, g, b):
    xf = x.astype(jnp.float32)
    mu = jnp.mean(xf, axis=-1, keepdims=True)
    var = jnp.mean(jnp.square(xf - mu), axis=-1, keepdims=True)
    return ((xf - mu) * lax.rsqrt(var + LN_EPS) * g + b).astype(x.dtype)


def _conv_group(u_a, u_b, dw, dw_b, ln_g, ln_b):
    h = u_a * jax.nn.sigmoid(u_b)
    h = lax.conv_general_dilated(
        h, dw[:, None, :].astype(h.dtype), window_strides=(1,),
        padding=[(CONV_WIDTH - 1, 0)],
        dimension_numbers=('NWC', 'WIO', 'NWC'),
        feature_group_count=CONV_CH) + dw_b
    h = _layernorm(h, ln_g, ln_b)
    return jax.nn.silu(h)


def _gla_group(q, k, v, lr, w_gate, gate_b, norm_g):
    B, S, _ = q.shape
    n_chunks = S // CHUNK
    log_a = jax.nn.log_sigmoid((lr @ w_gate + gate_b).astype(jnp.float32)) / GATE_TAU

    def chunks(t, d):
        return t.astype(jnp.float32).reshape(B, n_chunks, CHUNK, GLA_HEADS, d).transpose(1, 0, 3, 2, 4)

    qc = chunks(q, GLA_DK) * (GLA_DK ** -0.5)
    kc = chunks(k, GLA_DK)
    vc = chunks(v, GLA_DV)
    bcum = jnp.cumsum(chunks(log_a, GLA_DK), axis=3)
    b_last = bcum[:, :, :, -1:, :]
    q_t = qc * jnp.exp(bcum)
    k_t = kc * jnp.exp(-bcum)
    k_s = kc * jnp.exp(b_last - bcum)

    mask = jnp.tril(jnp.ones((CHUNK, CHUNK), dtype=bool))
    attn = jnp.where(mask, jnp.einsum('nbhcd,nbhsd->nbhcs', q_t, k_t), 0.0)
    o_intra = jnp.einsum('nbhcs,nbhse->nbhce', attn, vc)

    def step(state, inp):
        qt, ks, vv, bl = inp
        o = jnp.einsum('bhcd,bhde->bhce', qt, state)
        state = state * jnp.exp(bl)[:, :, 0, :, None] + jnp.einsum('bhcd,bhce->bhde', ks, vv)
        return state, o

    s0 = jnp.zeros((B, GLA_HEADS, GLA_DK, GLA_DV), jnp.float32)
    _, o_inter = lax.scan(step, s0, (q_t, k_s, vc, b_last))
    o = o_intra + o_inter
    o = o * lax.rsqrt(jnp.mean(jnp.square(o), axis=-1, keepdims=True) + LN_EPS) * norm_g
    return o.transpose(1, 0, 3, 2, 4).reshape(B, S, GLA_VAL).astype(v.dtype)


def _memory_group(q, mem_kv):
    B, S, _ = q.shape
    M = mem_kv.shape[1]
    mk, mv = jnp.split(mem_kv, 2, axis=-1)
    qh = q.reshape(B, S, MEM_HEADS, MEM_HD)
    kh = mk.reshape(B, M, MEM_HEADS, MEM_HD)
    vh = mv.reshape(B, M, MEM_HEADS, MEM_HD)
    scores = jnp.einsum('bshd,bmhd->bhsm', qh, kh).astype(jnp.float32) * (MEM_HD ** -0.5)
    p = jax.nn.softmax(scores, axis=-1).astype(vh.dtype)
    return jnp.einsum('bhsm,bmhd->bshd', p, vh).reshape(B, S, MEM_W)


def setup_inputs(seed: int = 0) -> dict:
    key = jax.random.key(seed)
    ks = jax.random.split(key, 14)
    nrm = lambda k, shape, s: jax.random.normal(k, shape, jnp.float32) * s
    return {
        'x': nrm(ks[0], (BATCH, SEQ, D_MODEL), 1.0),
        'mem': nrm(ks[1], (BATCH, N_MEM, D_MODEL), 1.0),
        'w_in': nrm(ks[2], (DEPTH, D_MODEL, D_IN), D_MODEL ** -0.5),
        'conv_dw': nrm(ks[3], (DEPTH, CONV_WIDTH, CONV_CH), CONV_WIDTH ** -0.5),
        'conv_dw_b': nrm(ks[4], (DEPTH, CONV_CH), 0.01),
        'conv_ln_g': 1.0 + nrm(ks[5], (DEPTH, CONV_CH), 0.01),
        'conv_ln_b': nrm(ks[6], (DEPTH, CONV_CH), 0.01),
        'gla_w_gate': nrm(ks[7], (DEPTH, GATE_RANK, GLA_KEY), GATE_RANK ** -0.5),
        'gla_gate_b': nrm(ks[8], (DEPTH, GLA_KEY), 0.01),
        'gla_norm_g': 1.0 + nrm(ks[9], (DEPTH, GLA_DV), 0.01),
        'w_mem_kv': nrm(ks[10], (DEPTH, D_MODEL, 2 * MEM_W), D_MODEL ** -0.5),
        'w_out': nrm(ks[11], (DEPTH, D_MIX, D_MODEL), (D_MIX ** -0.5) * DN_BETA),
        'ln_g': 1.0 + nrm(ks[12], (DEPTH, D_MODEL), 0.01),
        'ln_b': nrm(ks[13], (DEPTH, D_MODEL), 0.01),
    }


def reference(x, mem, w_in, conv_dw, conv_dw_b, conv_ln_g, conv_ln_b, gla_w_gate,
              gla_gate_b, gla_norm_g, w_mem_kv, w_out, ln_g, ln_b):
    split_idx = [int(i) for i in np.cumsum(SPLITS)[:-1]]
    for l in range(DEPTH):
        proj = x @ w_in[l]
        (c_a, c_b, c_gate, g_q, g_k, g_v, g_lr, g_gate, m_q, m_gate) = jnp.split(proj, split_idx, axis=-1)
        y_conv = _conv_group(c_a, c_b, conv_dw[l], conv_dw_b[l], conv_ln_g[l], conv_ln_b[l]) * jax.nn.silu(c_gate)
        y_gla = _gla_group(g_q, g_k, g_v, g_lr, gla_w_gate[l], gla_gate_b[l], gla_norm_g[l]) * jax.nn.silu(g_gate)
        y_mem = _memory_group(m_q, mem @ w_mem_kv[l]) * jax.nn.silu(m_gate)
        mixed = jnp.concatenate([y_conv, y_gla, y_mem], axis=-1)
        out = mixed @ w_out[l]
        x = _layernorm(DN_ALPHA * x + out, ln_g[l], ln_b[l])
    return x
```

```python
import functools

import jax
import jax.numpy as jnp
from jax import lax
from jax.experimental import pallas as pl
from jax.experimental.pallas import tpu as pltpu

D_MODEL = 4096
N_MEM = 256
MEM_HEADS = 4
MEM_W = D_MODEL // 4
MEM_HD = MEM_W // MEM_HEADS
CONV_CH = D_MODEL // 4
CONV_WIDTH = 31
GLA_HEADS = 4
GLA_VAL = D_MODEL // 2
GLA_DV = GLA_VAL // GLA_HEADS
GLA_DK = GLA_DV // 2
GLA_KEY = GLA_HEADS * GLA_DK
GATE_RANK = 16
GATE_TAU = 16.0
CHUNK = 64
LN_EPS = 1e-5
DEPTH = 1
DN_ALPHA = (2 * DEPTH) ** 0.25

LANES = 128
SUBLANES = 8
VMEM_LIMIT = 56 * 1024 * 1024

HALO = 32
CONV_ROWS = 64
BF16 = jnp.bfloat16
F32 = jnp.float32


def _dot(a, b):
    return jnp.dot(a, b, preferred_element_type=F32)


def _dot_nt(a, b):
    return lax.dot_general(a, b, (((1,), (1,)), ((), ())), preferred_element_type=F32)


def _sigmoid(x):
    return 1.0 / (1.0 + jnp.exp(-x))


def _silu(x):
    return x * _sigmoid(x)


def _memkv_kernel(m_ref, w_ref, o_ref):
    o_ref[...] = _dot(m_ref[...], w_ref[...]).astype(o_ref.dtype)


def _memkv(mem_bf, w_bf, tn=512):
    m, k = mem_bf.shape
    n = w_bf.shape[1]
    return pl.pallas_call(
        _memkv_kernel,
        grid=(n // tn,),
        in_specs=[pl.BlockSpec((m, k), lambda j: (0, 0)),
                  pl.BlockSpec((k, tn), lambda j: (0, j))],
        out_specs=pl.BlockSpec((m, tn), lambda j: (0, j)),
        out_shape=jax.ShapeDtypeStruct((m, n), BF16),
        compiler_params=pltpu.CompilerParams(
            dimension_semantics=("arbitrary",), vmem_limit_bytes=VMEM_LIMIT),
        name="memkv_proj",
    )(mem_bf, w_bf)


def _conv_kernel(x_ref, w_ref, dw_ref, dwb_ref, g_ref, b_ref, o_ref, a_s, h_s, *, tm, ch):
    i = pl.program_id(1)
    j = pl.program_id(2)

    @pl.when(j == 0)
    def _():
        a_s[...] = _dot(x_ref[...], w_ref[...])

    @pl.when(j == 1)
    def _():
        @pl.when(i == 0)
        def _():
            h_s[0:HALO, :] = jnp.zeros((HALO, ch), F32)

        @pl.when(i > 0)
        def _():
            h_s[0:HALO, :] = h_s[tm:tm + HALO, :]

        h_s[HALO:HALO + tm, :] = a_s[...] * _sigmoid(_dot(x_ref[...], w_ref[...]))

    @pl.when(j == 2)
    def _():
        a_s[...] = _silu(_dot(x_ref[...], w_ref[...]))
        n_win = CONV_ROWS + HALO

        def row_block(rb, carry):
            t0 = pl.multiple_of(rb * CONV_ROWS, CONV_ROWS)
            parts = []
            for c0 in range(0, ch, LANES):
                win = h_s[pl.ds(t0, n_win), c0:c0 + LANES]
                acc = jnp.zeros((CONV_ROWS, LANES), F32)
                for r in range(SUBLANES):
                    shifted = win if r == 0 else pltpu.roll(win, n_win - r, axis=0)
                    for q in range(HALO // SUBLANES + 1):
                        tap = q * SUBLANES + r - (HALO - CONV_WIDTH + 1)
                        if 0 <= tap < CONV_WIDTH:
                            acc = acc + dw_ref[tap:tap + 1, c0:c0 + LANES] * \
                                shifted[q * SUBLANES:q * SUBLANES + CONV_ROWS, :]
                parts.append(acc)
            hc = jnp.concatenate(parts, axis=1) + dwb_ref[...]
            mu = jnp.mean(hc, axis=-1, keepdims=True)
            d = hc - mu
            var = jnp.mean(d * d, axis=-1, keepdims=True)
            y = d * lax.rsqrt(var + LN_EPS) * g_ref[...] + b_ref[...]
            y = _silu(y) * a_s[pl.ds(t0, CONV_ROWS), :]
            o_ref[pl.ds(t0, CONV_ROWS), :] = y.astype(o_ref.dtype)
            return carry

        lax.fori_loop(0, tm // CONV_ROWS, row_block, 0)


def _conv_group(x_bf, w_conv, dw, dwb, g, b, *, batch, seq, tm):
    t, k = x_bf.shape
    ch = dw.shape[1]
    nt = seq // tm
    return pl.pallas_call(
        functools.partial(_conv_kernel, tm=tm, ch=ch),
        grid=(batch, nt, 3),
        in_specs=[pl.BlockSpec((tm, k), lambda bb, i, j: (bb * nt + i, 0)),
                  pl.BlockSpec((k, ch), lambda bb, i, j: (0, j)),
                  pl.BlockSpec((CONV_WIDTH, ch), lambda bb, i, j: (0, 0)),
                  pl.BlockSpec((1, ch), lambda bb, i, j: (0, 0)),
                  pl.BlockSpec((1, ch), lambda bb, i, j: (0, 0)),
                  pl.BlockSpec((1, ch), lambda bb, i, j: (0, 0))],
        out_specs=pl.BlockSpec((tm, ch), lambda bb, i, j: (bb * nt + i, 0)),
        out_shape=jax.ShapeDtypeStruct((t, ch), BF16),
        scratch_shapes=[pltpu.VMEM((tm, ch), F32),
                        pltpu.VMEM((tm + HALO, ch), F32)],
        compiler_params=pltpu.CompilerParams(
            dimension_semantics=("arbitrary", "arbitrary", "arbitrary"),
            vmem_limit_bytes=VMEM_LIMIT),
        name="conv_group",
    )(x_bf, w_conv, dw, dwb, g, b)


def _log_sigmoid(z):
    return jnp.minimum(z, 0.0) - jnp.log(1.0 + jnp.exp(-jnp.abs(z)))


def _gla_kernel(x_ref, w_ref, wlr_ref, wg_ref, gb_ref, ng_ref, o_ref, lr_s, st_s, *, tm, dk, dv):
    i = pl.program_id(1)
    h = pl.program_id(2)
    nc = tm // CHUNK
    x = x_ref[...]

    @pl.when(h == 0)
    def _():
        lr_s[...] = _dot(x, wlr_ref[...]).astype(BF16)

    @pl.when(i == 0)
    def _():
        st_s[h] = jnp.zeros((dk, dv), F32)

    q = _dot(x, w_ref[:, 0:dk]) * (dk ** -0.5)
    k = _dot(x, w_ref[:, dk:2 * dk])
    v = _dot(x, w_ref[:, 2 * dk:2 * dk + dv]).astype(BF16)

    z = _dot(lr_s[...], wg_ref[...]) + gb_ref[...]
    bcum = _log_sigmoid(z) / GATE_TAU
    row = lax.broadcasted_iota(jnp.int32, (tm, dk), 0) & (CHUNK - 1)
    s = 1
    while s < CHUNK:
        bcum = bcum + jnp.where(row >= s, pltpu.roll(bcum, s, axis=0), 0.0)
        s *= 2
    b3 = bcum.reshape(nc, CHUNK, dk)
    b_last = b3[:, CHUNK - 1:CHUNK, :]
    q_t = (q * jnp.exp(bcum)).astype(BF16)
    k_t = (k * jnp.exp(-bcum)).astype(BF16)
    k_s = k.reshape(nc, CHUNK, dk) * jnp.exp(b_last - b3)
    decay_t = jnp.exp(b_last.reshape(nc, dk)).T

    tri = lax.broadcasted_iota(jnp.int32, (CHUNK, CHUNK), 0) >= \
        lax.broadcasted_iota(jnp.int32, (CHUNK, CHUNK), 1)
    gate = _silu(_dot(x, w_ref[:, 2 * dk + dv:2 * dk + 2 * dv]))
    state = st_s[h]
    for c in range(nc):
        lo = c * CHUNK
        qc = q_t[lo:lo + CHUNK]
        vc = v[lo:lo + CHUNK]
        att = jnp.where(tri, _dot_nt(qc, k_t[lo:lo + CHUNK]), 0.0).astype(BF16)
        o = _dot(att, vc) + _dot(qc, state.astype(BF16))
        state = state * decay_t[:, c:c + 1] + _dot(k_s[c].T.astype(BF16), vc)
        o = o * lax.rsqrt(jnp.mean(o * o, axis=-1, keepdims=True) + LN_EPS) * ng_ref[...]
        o_ref[lo:lo + CHUNK, :] = (o * gate[lo:lo + CHUNK]).astype(o_ref.dtype)
    st_s[h] = state


def _gla_group(x_bf, w_gla, w_lr, w_gate, gate_b, norm_g, *, batch, seq, tm):
    t, kdim = x_bf.shape
    heads, dk, dv = GLA_HEADS, GLA_DK, GLA_DV
    hw = 2 * dk + 2 * dv
    nt = seq // tm
    return pl.pallas_call(
        functools.partial(_gla_kernel, tm=tm, dk=dk, dv=dv),
        grid=(batch, nt, heads),
        in_specs=[pl.BlockSpec((tm, kdim), lambda bb, i, h: (bb * nt + i, 0)),
                  pl.BlockSpec((kdim, hw), lambda bb, i, h: (0, h)),
                  pl.BlockSpec((kdim, LANES), lambda bb, i, h: (0, 0)),
                  pl.BlockSpec((LANES, dk), lambda bb, i, h: (0, h)),
                  pl.BlockSpec((1, dk), lambda bb, i, h: (0, h)),
                  pl.BlockSpec((1, dv), lambda bb, i, h: (0, 0))],
        out_specs=pl.BlockSpec((tm, dv), lambda bb, i, h: (bb * nt + i, h)),
        out_shape=jax.ShapeDtypeStruct((t, heads * dv), BF16),
        scratch_shapes=[pltpu.VMEM((tm, LANES), BF16),
                        pltpu.VMEM((heads, dk, dv), F32)],
        compiler_params=pltpu.CompilerParams(
            dimension_semantics=("arbitrary", "arbitrary", "arbitrary"),
            vmem_limit_bytes=VMEM_LIMIT),
        name="gla_group",
    )(x_bf, w_gla, w_lr, w_gate, gate_b, norm_g)


def _mem_kernel(x_ref, w_ref, k_ref, v_ref, o_ref, *, hd):
    x = x_ref[...]
    q = (_dot(x, w_ref[:, 0:hd])).astype(BF16)
    s = _dot_nt(q, k_ref[0]) * (hd ** -0.5)
    s = s - jnp.max(s, axis=-1, keepdims=True)
    e = jnp.exp(s)
    p = (e / jnp.sum(e, axis=-1, keepdims=True)).astype(BF16)
    o = _dot(p, v_ref[0])
    gate = _silu(_dot(x, w_ref[:, hd:2 * hd]))
    o_ref[...] = (o * gate).astype(o_ref.dtype)


def _mem_group(x_bf, w_mem, mem_kv, *, batch, seq, tm):
    t, kdim = x_bf.shape
    heads, hd = MEM_HEADS, MEM_HD
    nt = seq // tm
    n_mem = mem_kv.shape[1]
    return pl.pallas_call(
        functools.partial(_mem_kernel, hd=hd),
        grid=(batch, nt, heads),
        in_specs=[pl.BlockSpec((tm, kdim), lambda bb, i, h: (bb * nt + i, 0)),
                  pl.BlockSpec((kdim, 2 * hd), lambda bb, i, h: (0, h)),
                  pl.BlockSpec((1, n_mem, hd), lambda bb, i, h: (bb, 0, h)),
                  pl.BlockSpec((1, n_mem, hd), lambda bb, i, h: (bb, 0, heads + h))],
        out_specs=pl.BlockSpec((tm, hd), lambda bb, i, h: (bb * nt + i, h)),
        out_shape=jax.ShapeDtypeStruct((t, heads * hd), BF16),
        compiler_params=pltpu.CompilerParams(
            dimension_semantics=("arbitrary", "arbitrary", "arbitrary"),
            vmem_limit_bytes=VMEM_LIMIT),
        name="mem_group",
    )(x_bf, w_mem, mem_kv, mem_kv)


def _out_kernel(yc_ref, yg_ref, ym_ref, w_ref, x_ref, g_ref, b_ref, o_ref, *, tm, tn, n_col, rows):
    j = pl.program_id(1)
    c_w = yc_ref.shape[1]
    g_w = yg_ref.shape[1]
    acc = _dot(yc_ref[...], w_ref[0:c_w, :])
    acc = acc + _dot(yg_ref[...], w_ref[c_w:c_w + g_w, :])
    acc = acc + _dot(ym_ref[...], w_ref[c_w + g_w:, :])
    col = pl.multiple_of(j * tn, tn)
    o_ref[:, pl.ds(col, tn)] = acc + DN_ALPHA * x_ref[...]

    @pl.when(j == n_col - 1)
    def _():
        def row_block(rb, carry):
            t0 = pl.multiple_of(rb * rows, rows)
            r = o_ref[pl.ds(t0, rows), :]
            mu = jnp.mean(r, axis=-1, keepdims=True)
            d = r - mu
            var = jnp.mean(d * d, axis=-1, keepdims=True)
            o_ref[pl.ds(t0, rows), :] = d * lax.rsqrt(var + LN_EPS) * g_ref[...] + b_ref[...]
            return carry

        lax.fori_loop(0, tm // rows, row_block, 0)


def _out_proj(y_conv, y_gla, y_mem, w_out, x2, ln_g, ln_b, *, tm, tn):
    t, d = x2.shape
    n_col = d // tn
    kdim = w_out.shape[0]
    return pl.pallas_call(
        functools.partial(_out_kernel, tm=tm, tn=tn, n_col=n_col, rows=32),
        grid=(t // tm, n_col),
        in_specs=[pl.BlockSpec((tm, y_conv.shape[1]), lambda i, j: (i, 0)),
                  pl.BlockSpec((tm, y_gla.shape[1]), lambda i, j: (i, 0)),
                  pl.BlockSpec((tm, y_mem.shape[1]), lambda i, j: (i, 0)),
                  pl.BlockSpec((kdim, tn), lambda i, j: (0, j)),
                  pl.BlockSpec((tm, tn), lambda i, j: (i, j)),
                  pl.BlockSpec((1, d), lambda i, j: (0, 0)),
                  pl.BlockSpec((1, d), lambda i, j: (0, 0))],
        out_specs=pl.BlockSpec((tm, d), lambda i, j: (i, 0)),
        out_shape=jax.ShapeDtypeStruct((t, d), F32),
        compiler_params=pltpu.CompilerParams(
            dimension_semantics=("arbitrary", "arbitrary"), vmem_limit_bytes=VMEM_LIMIT),
        name="out_proj_ln",
    )(y_conv, y_gla, y_mem, w_out, x2, ln_g, ln_b)


def _split_w_in(w):
    o = 0
    parts = {}
    for name, width in (("c_a", CONV_CH), ("c_b", CONV_CH), ("c_gate", CONV_CH),
                        ("g_q", GLA_KEY), ("g_k", GLA_KEY), ("g_v", GLA_VAL), ("g_lr", GATE_RANK),
                        ("g_gate", GLA_VAL), ("m_q", MEM_W), ("m_gate", MEM_W)):
        parts[name] = w[:, o:o + width].astype(BF16)
        o += width
    w_conv = jnp.concatenate([parts["c_a"], parts["c_b"], parts["c_gate"]], axis=1)
    gla_cols = []
    for h in range(GLA_HEADS):
        gla_cols += [parts["g_q"][:, h * GLA_DK:(h + 1) * GLA_DK],
                     parts["g_k"][:, h * GLA_DK:(h + 1) * GLA_DK],
                     parts["g_v"][:, h * GLA_DV:(h + 1) * GLA_DV],
                     parts["g_gate"][:, h * GLA_DV:(h + 1) * GLA_DV]]
    w_gla = jnp.concatenate(gla_cols, axis=1)
    w_lr = jnp.pad(parts["g_lr"], ((0, 0), (0, LANES - GATE_RANK)))
    mem_cols = []
    for h in range(MEM_HEADS):
        mem_cols += [parts["m_q"][:, h * MEM_HD:(h + 1) * MEM_HD],
                     parts["m_gate"][:, h * MEM_HD:(h + 1) * MEM_HD]]
    w_mem = jnp.concatenate(mem_cols, axis=1)
    return w_conv, w_gla, w_lr, w_mem


def kernel(x, mem, w_in, conv_dw, conv_dw_b, conv_ln_g, conv_ln_b, gla_w_gate, gla_gate_b,
           gla_norm_g, w_mem_kv, w_out, ln_g, ln_b):
    batch, seq, d = x.shape
    t = batch * seq
    x2 = x.reshape(t, d)
    for l in range(DEPTH):
        x_bf = x2.astype(BF16)
        w_conv, w_gla, w_lr, w_mem = _split_w_in(w_in[l])
        w_gate = jnp.pad(gla_w_gate[l].astype(BF16), ((0, LANES - GATE_RANK), (0, 0)))
        mem_kv = _memkv(mem.reshape(batch * N_MEM, d).astype(BF16), w_mem_kv[l].astype(BF16))
        mem_kv = mem_kv.reshape(batch, N_MEM, 2 * MEM_W)
        y_conv = _conv_group(x_bf, w_conv, conv_dw[l], conv_dw_b[l][None], conv_ln_g[l][None],
                             conv_ln_b[l][None], batch=batch, seq=seq, tm=512)
        y_gla = _gla_group(x_bf, w_gla, w_lr, w_gate, gla_gate_b[l][None], gla_norm_g[l][None],
                           batch=batch, seq=seq, tm=512)
        y_mem = _mem_group(x_bf, w_mem, mem_kv, batch=batch, seq=seq, tm=512)
        x2 = _out_proj(y_conv, y_gla, y_mem, w_out[l].astype(BF16), x2, ln_g[l][None],
                       ln_b[l][None], tm=512, tn=1024)
    return x2.reshape(batch, seq, d)
```

```python
import functools

import jax
import jax.numpy as jnp
from jax import lax
from jax.experimental import pallas as pl
from jax.experimental.pallas import tpu as pltpu

D_MODEL = 4096
N_MEM = 256
MEM_HEADS = 4
MEM_W = D_MODEL // 4
MEM_HD = MEM_W // MEM_HEADS
CONV_CH = D_MODEL // 4
CONV_WIDTH = 31
GLA_HEADS = 4
GLA_VAL = D_MODEL // 2
GLA_DV = GLA_VAL // GLA_HEADS
GLA_DK = GLA_DV // 2
GLA_KEY = GLA_HEADS * GLA_DK
GATE_RANK = 16
GATE_TAU = 16.0
CHUNK = 64
LN_EPS = 1e-5
DEPTH = 1
DN_ALPHA = (2 * DEPTH) ** 0.25

LANES = 128
SUBLANES = 8
VMEM_LIMIT = 56 * 1024 * 1024

HALO = 32
CONV_ROWS = 64
LN_ROWS = 64
BF16 = jnp.bfloat16
F32 = jnp.float32


def _dot(a, b):
    return jnp.dot(a, b, preferred_element_type=F32)


def _dot_nt(a, b):
    return lax.dot_general(a, b, (((1,), (1,)), ((), ())), preferred_element_type=F32)


def _sigmoid(x):
    return 1.0 / (1.0 + jnp.exp(-x))


def _silu(x):
    return x * _sigmoid(x)


def _memkv_kernel(m_ref, w_ref, o_ref):
    o_ref[...] = _dot(m_ref[...].astype(BF16), w_ref[...].astype(BF16)).astype(o_ref.dtype)


def _memkv(mem2, w, tn=512):
    m, k = mem2.shape
    n = w.shape[1]
    return pl.pallas_call(
        _memkv_kernel,
        grid=(n // tn,),
        in_specs=[pl.BlockSpec((m, k), lambda j: (0, 0)),
                  pl.BlockSpec((k, tn), lambda j: (0, j))],
        out_specs=pl.BlockSpec((m, tn), lambda j: (0, j)),
        out_shape=jax.ShapeDtypeStruct((m, n), BF16),
        compiler_params=pltpu.CompilerParams(
            dimension_semantics=("arbitrary",), vmem_limit_bytes=VMEM_LIMIT),
        name="memkv_proj",
    )(mem2, w)


def _mem_kernel(x_ref, w_ref, k_ref, v_ref, o_ref, xb_ref, *, hd):
    @pl.when(pl.program_id(2) == 0)
    def _():
        xb_ref[...] = x_ref[...].astype(BF16)

    qg = _dot(xb_ref[...], w_ref[...])
    q = qg[:, 0:hd].astype(BF16)
    s = _dot_nt(q, k_ref[0]) * (hd ** -0.5)
    s = s - jnp.max(s, axis=-1, keepdims=True)
    e = jnp.exp(s)
    p = (e / jnp.sum(e, axis=-1, keepdims=True)).astype(BF16)
    o = _dot(p, v_ref[0])
    o_ref[...] = (o * _silu(qg[:, hd:2 * hd])).astype(o_ref.dtype)


def _mem_group(x2, w_mem, mem_kv, *, batch, seq, tm):
    t, kdim = x2.shape
    heads, hd = MEM_HEADS, MEM_HD
    nt = seq // tm
    n_mem = mem_kv.shape[1]
    return pl.pallas_call(
        functools.partial(_mem_kernel, hd=hd),
        grid=(batch, nt, heads),
        in_specs=[pl.BlockSpec((tm, kdim), lambda bb, i, h: (bb * nt + i, 0)),
                  pl.BlockSpec((kdim, 2 * hd), lambda bb, i, h: (0, h)),
                  pl.BlockSpec((1, n_mem, hd), lambda bb, i, h: (bb, 0, h)),
                  pl.BlockSpec((1, n_mem, hd), lambda bb, i, h: (bb, 0, heads + h))],
        out_specs=[pl.BlockSpec((tm, hd), lambda bb, i, h: (bb * nt + i, h)),
                   pl.BlockSpec((tm, kdim), lambda bb, i, h: (bb * nt + i, 0))],
        out_shape=[jax.ShapeDtypeStruct((t, heads * hd), BF16),
                   jax.ShapeDtypeStruct((t, kdim), BF16)],
        compiler_params=pltpu.CompilerParams(
            dimension_semantics=("arbitrary", "arbitrary", "arbitrary"),
            vmem_limit_bytes=VMEM_LIMIT),
        name="mem_group",
    )(x2, w_mem, mem_kv, mem_kv)


def _conv_kernel(x_ref, w_ref, dw_ref, dwb_ref, g_ref, b_ref, o_ref, a_s, h_s, *, tm, ch):
    i = pl.program_id(1)
    j = pl.program_id(2)

    @pl.when(j == 0)
    def _():
        a_s[...] = _dot(x_ref[...], w_ref[...])

    @pl.when(j == 1)
    def _():
        tail = h_s[tm:tm + HALO, :]
        h_s[0:HALO, :] = jnp.where(i == 0, 0.0, tail)
        h_s[HALO:HALO + tm, :] = a_s[...] * _sigmoid(_dot(x_ref[...], w_ref[...]))

    @pl.when(j == 2)
    def _():
        a_s[...] = _silu(_dot(x_ref[...], w_ref[...]))
        n_win = CONV_ROWS + HALO

        def row_block(rb, carry):
            t0 = pl.multiple_of(rb * CONV_ROWS, CONV_ROWS)
            parts = []
            for c0 in range(0, ch, LANES):
                win = h_s[pl.ds(t0, n_win), c0:c0 + LANES]
                acc = jnp.zeros((CONV_ROWS, LANES), F32)
                for r in range(SUBLANES):
                    shifted = win if r == 0 else pltpu.roll(win, n_win - r, axis=0)
                    for q in range(HALO // SUBLANES + 1):
                        tap = q * SUBLANES + r - (HALO - CONV_WIDTH + 1)
                        if 0 <= tap < CONV_WIDTH:
                            acc = acc + dw_ref[tap:tap + 1, c0:c0 + LANES] * \
                                shifted[q * SUBLANES:q * SUBLANES + CONV_ROWS, :]
                parts.append(acc)
            hc = jnp.concatenate(parts, axis=1) + dwb_ref[...]
            mu = jnp.mean(hc, axis=-1, keepdims=True)
            d = hc - mu
            var = jnp.mean(d * d, axis=-1, keepdims=True)
            y = d * lax.rsqrt(var + LN_EPS) * g_ref[...] + b_ref[...]
            y = _silu(y) * a_s[pl.ds(t0, CONV_ROWS), :]
            o_ref[pl.ds(t0, CONV_ROWS), :] = y.astype(o_ref.dtype)
            return carry

        lax.fori_loop(0, tm // CONV_ROWS, row_block, 0)


def _conv_group(x_bf, w_conv, dw, dwb, g, b, *, batch, seq, tm):
    t, k = x_bf.shape
    ch = dw.shape[1]
    nt = seq // tm
    return pl.pallas_call(
        functools.partial(_conv_kernel, tm=tm, ch=ch),
        grid=(batch, nt, 3),
        in_specs=[pl.BlockSpec((tm, k), lambda bb, i, j: (bb * nt + i, 0)),
                  pl.BlockSpec((k, ch), lambda bb, i, j: (0, j)),
                  pl.BlockSpec((CONV_WIDTH, ch), lambda bb, i, j: (0, 0)),
                  pl.BlockSpec((1, ch), lambda bb, i, j: (0, 0)),
                  pl.BlockSpec((1, ch), lambda bb, i, j: (0, 0)),
                  pl.BlockSpec((1, ch), lambda bb, i, j: (0, 0))],
        out_specs=pl.BlockSpec((tm, ch), lambda bb, i, j: (bb * nt + i, 0)),
        out_shape=jax.ShapeDtypeStruct((t, ch), BF16),
        scratch_shapes=[pltpu.VMEM((tm, ch), F32),
                        pltpu.VMEM((tm + HALO, ch), F32)],
        compiler_params=pltpu.CompilerParams(
            dimension_semantics=("arbitrary", "arbitrary", "arbitrary"),
            vmem_limit_bytes=VMEM_LIMIT),
        name="conv_group",
    )(x_bf, w_conv, dw, dwb, g, b)


def _log_sigmoid(z):
    return jnp.minimum(z, 0.0) - jnp.log(1.0 + jnp.exp(-jnp.abs(z)))


def _gla_kernel(x_ref, wqk_ref, wv_ref, wgg_ref, wlr_ref, wg_ref, gb_ref, ng_ref, o_ref, lr_s, st_s,
                *, tm, dk, dv):
    i = pl.program_id(1)
    h = pl.program_id(2)
    nc = tm // CHUNK
    x = x_ref[...]

    @pl.when(h == 0)
    def _():
        lr_s[...] = _dot(x, wlr_ref[...]).astype(BF16)

    @pl.when(i == 0)
    def _():
        st_s[h] = jnp.zeros((dk, dv), F32)

    qk = _dot(x, wqk_ref[...])
    q = qk[:, 0:dk] * (dk ** -0.5)
    k = qk[:, dk:2 * dk]
    v = _dot(x, wv_ref[...]).astype(BF16)

    z = _dot(lr_s[...], wg_ref[...]) + gb_ref[...]
    bcum = _log_sigmoid(z) / GATE_TAU
    row = lax.broadcasted_iota(jnp.int32, (tm, dk), 0) & (CHUNK - 1)
    s = 1
    while s < CHUNK:
        bcum = bcum + jnp.where(row >= s, pltpu.roll(bcum, s, axis=0), 0.0)
        s *= 2
    b3 = bcum.reshape(nc, CHUNK, dk)
    b_last = b3[:, CHUNK - 1:CHUNK, :]
    q_t = (q * jnp.exp(bcum)).astype(BF16)
    k_t = (k * jnp.exp(-bcum)).astype(BF16)
    k_s = k.reshape(nc, CHUNK, dk) * jnp.exp(b_last - b3)
    decay_t = jnp.exp(b_last.reshape(nc, dk)).T

    tri = lax.broadcasted_iota(jnp.int32, (CHUNK, CHUNK), 0) >= \
        lax.broadcasted_iota(jnp.int32, (CHUNK, CHUNK), 1)
    gate = _silu(_dot(x, wgg_ref[...]))
    state = st_s[h]
    for c in range(nc):
        lo = c * CHUNK
        qc = q_t[lo:lo + CHUNK]
        vc = v[lo:lo + CHUNK]
        att = jnp.where(tri, _dot_nt(qc, k_t[lo:lo + CHUNK]), 0.0).astype(BF16)
        o = _dot(att, vc) + _dot(qc, state.astype(BF16))
        state = state * decay_t[:, c:c + 1] + _dot(k_s[c].T.astype(BF16), vc)
        o = o * lax.rsqrt(jnp.mean(o * o, axis=-1, keepdims=True) + LN_EPS) * ng_ref[...]
        o_ref[lo:lo + CHUNK, :] = (o * gate[lo:lo + CHUNK]).astype(o_ref.dtype)
    st_s[h] = state


def _gla_group(x_bf, w_qk, w_v, w_gg, w_lr, w_gate, gate_b, norm_g, *, batch, seq, tm):
    t, kdim = x_bf.shape
    heads, dk, dv = GLA_HEADS, GLA_DK, GLA_DV
    nt = seq // tm
    return pl.pallas_call(
        functools.partial(_gla_kernel, tm=tm, dk=dk, dv=dv),
        grid=(batch, nt, heads),
        in_specs=[pl.BlockSpec((tm, kdim), lambda bb, i, h: (bb * nt + i, 0)),
                  pl.BlockSpec((kdim, 2 * dk), lambda bb, i, h: (0, h)),
                  pl.BlockSpec((kdim, dv), lambda bb, i, h: (0, h)),
                  pl.BlockSpec((kdim, dv), lambda bb, i, h: (0, h)),
                  pl.BlockSpec((kdim, LANES), lambda bb, i, h: (0, 0)),
                  pl.BlockSpec((LANES, dk), lambda bb, i, h: (0, h)),
                  pl.BlockSpec((1, dk), lambda bb, i, h: (0, h)),
                  pl.BlockSpec((1, dv), lambda bb, i, h: (0, 0))],
        out_specs=pl.BlockSpec((tm, dv), lambda bb, i, h: (bb * nt + i, h)),
        out_shape=jax.ShapeDtypeStruct((t, heads * dv), BF16),
        scratch_shapes=[pltpu.VMEM((tm, LANES), BF16),
                        pltpu.VMEM((heads, dk, dv), F32)],
        compiler_params=pltpu.CompilerParams(
            dimension_semantics=("arbitrary", "arbitrary", "arbitrary"),
            vmem_limit_bytes=VMEM_LIMIT),
        name="gla_group",
    )(x_bf, w_qk, w_v, w_gg, w_lr, w_gate, gate_b, norm_g)


def _out_kernel(yc_ref, yg_ref, ym_ref, w_ref, x_ref, g_ref, b_ref, o_ref, run_mean, run_m2,
                *, tm, tn, n_col, d):
    j = pl.program_id(1)
    col = pl.multiple_of(j * tn, tn)
    c_w = yc_ref.shape[1]
    g_w = yg_ref.shape[1]
    acc = _dot(yc_ref[...], w_ref[0:c_w, :])
    acc = acc + _dot(yg_ref[...], w_ref[c_w:c_w + g_w, :])
    acc = acc + _dot(ym_ref[...], w_ref[c_w + g_w:, :])
    r = acc + DN_ALPHA * x_ref[...]
    o_ref[:, pl.ds(col, tn)] = r

    blk_mean = jnp.mean(r, axis=-1, keepdims=True)
    dev = r - blk_mean
    blk_m2 = jnp.sum(dev * dev, axis=-1, keepdims=True)
    n_a = (j * tn).astype(F32)
    n_ab = n_a + tn
    delta = blk_mean - run_mean[...]
    run_m2[...] = jnp.where(j == 0, blk_m2, run_m2[...] + blk_m2 + delta * delta * (n_a * tn / n_ab))
    run_mean[...] = jnp.where(j == 0, blk_mean, run_mean[...] + delta * (tn / n_ab))

    @pl.when(j == n_col - 1)
    def _():
        run_m2[...] = lax.rsqrt(run_m2[...] / d + LN_EPS)

        def row_block(rb, carry):
            t0 = pl.multiple_of(rb * LN_ROWS, LN_ROWS)
            rows = pl.ds(t0, LN_ROWS)
            o_ref[rows, :] = (o_ref[rows, :] - run_mean[rows, :]) * run_m2[rows, :] * g_ref[...] \
                + b_ref[...]
            return carry

        lax.fori_loop(0, tm // LN_ROWS, row_block, 0)


def _out_proj(y_conv, y_gla, y_mem, w_out, x2, ln_g, ln_b, *, tm, tn):
    t, d = x2.shape
    n_col = d // tn
    kdim = w_out.shape[0]
    stat = pltpu.VMEM((tm, 1), F32)
    return pl.pallas_call(
        functools.partial(_out_kernel, tm=tm, tn=tn, n_col=n_col, d=d),
        grid=(t // tm, n_col),
        in_specs=[pl.BlockSpec((tm, y_conv.shape[1]), lambda i, j: (i, 0)),
                  pl.BlockSpec((tm, y_gla.shape[1]), lambda i, j: (i, 0)),
                  pl.BlockSpec((tm, y_mem.shape[1]), lambda i, j: (i, 0)),
                  pl.BlockSpec((kdim, tn), lambda i, j: (0, j)),
                  pl.BlockSpec((tm, tn), lambda i, j: (i, j)),
                  pl.BlockSpec((1, d), lambda i, j: (0, 0)),
                  pl.BlockSpec((1, d), lambda i, j: (0, 0))],
        out_specs=pl.BlockSpec((tm, d), lambda i, j: (i, 0)),
        out_shape=jax.ShapeDtypeStruct((t, d), F32),
        scratch_shapes=[stat, stat],
        compiler_params=pltpu.CompilerParams(
            dimension_semantics=("arbitrary", "arbitrary"), vmem_limit_bytes=VMEM_LIMIT),
        name="out_proj_ln",
    )(y_conv, y_gla, y_mem, w_out, x2, ln_g, ln_b)


def _split_w_in(w):
    kdim = w.shape[0]

    def pair_heads(cols, heads, width):
        return cols.reshape(kdim, 2, heads, width).transpose(0, 2, 1, 3).reshape(kdim, 2 * heads * width)

    o_qk = 3 * CONV_CH
    o_v = o_qk + 2 * GLA_KEY
    o_lr = o_v + GLA_VAL
    o_gg = o_lr + GATE_RANK
    o_mem = o_gg + GLA_VAL
    w_conv = w[:, 0:o_qk].astype(BF16)
    w_qk = pair_heads(w[:, o_qk:o_v], GLA_HEADS, GLA_DK).astype(BF16)
    w_v = w[:, o_v:o_lr].astype(BF16)
    w_lr = jnp.pad(w[:, o_lr:o_gg], ((0, 0), (0, LANES - GATE_RANK))).astype(BF16)
    w_gg = w[:, o_gg:o_mem].astype(BF16)
    w_mem = pair_heads(w[:, o_mem:o_mem + 2 * MEM_W], MEM_HEADS, MEM_HD).astype(BF16)
    return w_conv, w_qk, w_v, w_lr, w_gg, w_mem


def kernel(x, mem, w_in, conv_dw, conv_dw_b, conv_ln_g, conv_ln_b, gla_w_gate, gla_gate_b,
           gla_norm_g, w_mem_kv, w_out, ln_g, ln_b):
    batch, seq, d = x.shape
    t = batch * seq
    x2 = x.reshape(t, d)
    for l in range(DEPTH):
        w_conv, w_qk, w_v, w_lr, w_gg, w_mem = _split_w_in(w_in[l])
        w_gate = jnp.pad(gla_w_gate[l].astype(BF16), ((0, LANES - GATE_RANK), (0, 0)))
        mem_kv = _memkv(mem.reshape(batch * N_MEM, d), w_mem_kv[l])
        mem_kv = mem_kv.reshape(batch, N_MEM, 2 * MEM_W)
        y_mem, x_bf = _mem_group(x2, w_mem, mem_kv, batch=batch, seq=seq, tm=512)
        y_conv = _conv_group(x_bf, w_conv, conv_dw[l], conv_dw_b[l][None], conv_ln_g[l][None],
                             conv_ln_b[l][None], batch=batch, seq=seq, tm=512)
        y_gla = _gla_group(x_bf, w_qk, w_v, w_gg, w_lr, w_gate, gla_gate_b[l][None],
                           gla_norm_g[l][None], batch=batch, seq=seq, tm=512)
        x2 = _out_proj(y_conv, y_gla, y_mem, w_out[l].astype(BF16), x2, ln_g[l][None],
                       ln_b[l][None], tm=512, tn=1024)
    return x2.reshape(batch, seq, d)
```

```python
import functools

import jax
import jax.numpy as jnp
from jax import lax
from jax.experimental import pallas as pl
from jax.experimental.pallas import tpu as pltpu

D_MODEL = 4096
N_MEM = 256
MEM_HEADS = 4
MEM_W = D_MODEL // 4
MEM_HD = MEM_W // MEM_HEADS
CONV_CH = D_MODEL // 4
CONV_WIDTH = 31
GLA_HEADS = 4
GLA_VAL = D_MODEL // 2
GLA_DV = GLA_VAL // GLA_HEADS
GLA_DK = GLA_DV // 2
GLA_KEY = GLA_HEADS * GLA_DK
GATE_RANK = 16
GATE_TAU = 16.0
CHUNK = 64
LN_EPS = 1e-5
DEPTH = 1
DN_ALPHA = (2 * DEPTH) ** 0.25

LANES = 128
SUBLANES = 8
VMEM_LIMIT = 56 * 1024 * 1024

HALO = 32
CONV_ROWS = 64
LN_ROWS = 64
BF16 = jnp.bfloat16
F32 = jnp.float32


def _dot(a, b):
    return jnp.dot(a, b, preferred_element_type=F32)


def _dot_nt(a, b):
    return lax.dot_general(a, b, (((1,), (1,)), ((), ())), preferred_element_type=F32)


def _sigmoid(x):
    return 1.0 / (1.0 + jnp.exp(-x))


def _silu(x):
    return x * _sigmoid(x)


def _memkv_kernel(m_ref, w_ref, o_ref):
    o_ref[...] = _dot(m_ref[...].astype(BF16), w_ref[...].astype(BF16)).astype(o_ref.dtype)


def _memkv(mem2, w, tn=512):
    m, k = mem2.shape
    n = w.shape[1]
    return pl.pallas_call(
        _memkv_kernel,
        grid=(n // tn,),
        in_specs=[pl.BlockSpec((m, k), lambda j: (0, 0)),
                  pl.BlockSpec((k, tn), lambda j: (0, j))],
        out_specs=pl.BlockSpec((m, tn), lambda j: (0, j)),
        out_shape=jax.ShapeDtypeStruct((m, n), BF16),
        compiler_params=pltpu.CompilerParams(
            dimension_semantics=("arbitrary",), vmem_limit_bytes=VMEM_LIMIT),
        name="memkv_proj",
    )(mem2, w)


def _mem_kernel(x_ref, wq_ref, wg_ref, k_ref, v_ref, o_ref, xb_ref, q_s, g_s, *, hd):
    h = pl.program_id(2)

    @pl.when(h == 0)
    def _():
        xb_ref[...] = x_ref[...].astype(BF16)

    @pl.when(lax.rem(h, 2) == 0)
    def _():
        q_s[...] = _dot(xb_ref[...], wq_ref[...]).astype(BF16)
        g_s[...] = _dot(xb_ref[...], wg_ref[...])

    half = pl.ds(pl.multiple_of(lax.rem(h, 2) * hd, hd), hd)
    s = _dot_nt(q_s[:, half], k_ref[0]) * (hd ** -0.5)
    s = s - jnp.max(s, axis=-1, keepdims=True)
    e = jnp.exp(s)
    p = (e / jnp.sum(e, axis=-1, keepdims=True)).astype(BF16)
    o = _dot(p, v_ref[0])
    o_ref[...] = (o * _silu(g_s[:, half])).astype(o_ref.dtype)


def _mem_group(x2, w_tail, mem_kv, *, q_col, gate_col, batch, seq, tm):
    t, kdim = x2.shape
    heads, hd = MEM_HEADS, MEM_HD
    nt = seq // tm
    n_mem = mem_kv.shape[1]
    q_blk, g_blk = q_col // (2 * hd), gate_col // (2 * hd)
    assert q_col % (2 * hd) == 0 and gate_col % (2 * hd) == 0 and heads % 2 == 0
    return pl.pallas_call(
        functools.partial(_mem_kernel, hd=hd),
        grid=(batch, nt, heads),
        in_specs=[pl.BlockSpec((tm, kdim), lambda bb, i, h: (bb * nt + i, 0)),
                  pl.BlockSpec((kdim, 2 * hd), lambda bb, i, h: (0, q_blk + h // 2)),
                  pl.BlockSpec((kdim, 2 * hd), lambda bb, i, h: (0, g_blk + h // 2)),
                  pl.BlockSpec((1, n_mem, hd), lambda bb, i, h: (bb, 0, h)),
                  pl.BlockSpec((1, n_mem, hd), lambda bb, i, h: (bb, 0, heads + h))],
        out_specs=[pl.BlockSpec((tm, hd), lambda bb, i, h: (bb * nt + i, h)),
                   pl.BlockSpec((tm, kdim), lambda bb, i, h: (bb * nt + i, 0))],
        out_shape=[jax.ShapeDtypeStruct((t, heads * hd), BF16),
                   jax.ShapeDtypeStruct((t, kdim), BF16)],
        scratch_shapes=[pltpu.VMEM((tm, 2 * hd), BF16),
                        pltpu.VMEM((tm, 2 * hd), F32)],
        compiler_params=pltpu.CompilerParams(
            dimension_semantics=("arbitrary", "arbitrary", "arbitrary"),
            vmem_limit_bytes=VMEM_LIMIT),
        name="mem_group",
    )(x2, w_tail, w_tail, mem_kv, mem_kv)


def _conv_kernel(x_ref, w_ref, dw_ref, dwb_ref, g_ref, b_ref, o_ref, a_s, h_s, *, tm, ch):
    i = pl.program_id(1)
    j = pl.program_id(2)

    @pl.when(j == 0)
    def _():
        a_s[...] = _dot(x_ref[...], w_ref[...])

    @pl.when(j == 1)
    def _():
        tail = h_s[tm:tm + HALO, :]
        h_s[0:HALO, :] = jnp.where(i == 0, 0.0, tail)
        h_s[HALO:HALO + tm, :] = a_s[...] * _sigmoid(_dot(x_ref[...], w_ref[...]))

    @pl.when(j == 2)
    def _():
        a_s[...] = _silu(_dot(x_ref[...], w_ref[...]))
        n_win = CONV_ROWS + HALO

        def row_block(rb, carry):
            t0 = pl.multiple_of(rb * CONV_ROWS, CONV_ROWS)
            parts = []
            for c0 in range(0, ch, LANES):
                win = h_s[pl.ds(t0, n_win), c0:c0 + LANES]
                acc = jnp.zeros((CONV_ROWS, LANES), F32)
                for r in range(SUBLANES):
                    shifted = win if r == 0 else pltpu.roll(win, n_win - r, axis=0)
                    for q in range(HALO // SUBLANES + 1):
                        tap = q * SUBLANES + r - (HALO - CONV_WIDTH + 1)
                        if 0 <= tap < CONV_WIDTH:
                            acc = acc + dw_ref[tap:tap + 1, c0:c0 + LANES] * \
                                shifted[q * SUBLANES:q * SUBLANES + CONV_ROWS, :]
                parts.append(acc)
            hc = jnp.concatenate(parts, axis=1) + dwb_ref[...]
            mu = jnp.mean(hc, axis=-1, keepdims=True)
            d = hc - mu
            var = jnp.mean(d * d, axis=-1, keepdims=True)
            y = d * lax.rsqrt(var + LN_EPS) * g_ref[...] + b_ref[...]
            y = _silu(y) * a_s[pl.ds(t0, CONV_ROWS), :]
            o_ref[pl.ds(t0, CONV_ROWS), :] = y.astype(o_ref.dtype)
            return carry

        lax.fori_loop(0, tm // CONV_ROWS, row_block, 0)


def _conv_group(x_bf, w_conv, dw, dwb, g, b, *, batch, seq, tm):
    t, k = x_bf.shape
    ch = dw.shape[1]
    nt = seq // tm
    return pl.pallas_call(
        functools.partial(_conv_kernel, tm=tm, ch=ch),
        grid=(batch, nt, 3),
        in_specs=[pl.BlockSpec((tm, k), lambda bb, i, j: (bb * nt + i, 0)),
                  pl.BlockSpec((k, ch), lambda bb, i, j: (0, j)),
                  pl.BlockSpec((CONV_WIDTH, ch), lambda bb, i, j: (0, 0)),
                  pl.BlockSpec((1, ch), lambda bb, i, j: (0, 0)),
                  pl.BlockSpec((1, ch), lambda bb, i, j: (0, 0)),
                  pl.BlockSpec((1, ch), lambda bb, i, j: (0, 0))],
        out_specs=pl.BlockSpec((tm, ch), lambda bb, i, j: (bb * nt + i, 0)),
        out_shape=jax.ShapeDtypeStruct((t, ch), BF16),
        scratch_shapes=[pltpu.VMEM((tm, ch), F32),
                        pltpu.VMEM((tm + HALO, ch), F32)],
        compiler_params=pltpu.CompilerParams(
            dimension_semantics=("arbitrary", "arbitrary", "arbitrary"),
            vmem_limit_bytes=VMEM_LIMIT),
        name="conv_group",
    )(x_bf, w_conv, dw, dwb, g, b)


def _log_sigmoid(z):
    return jnp.minimum(z, 0.0) - jnp.log(1.0 + jnp.exp(-jnp.abs(z)))


def _gla_kernel(x_ref, wq_ref, wk_ref, wv_ref, wgg_ref, wlr_ref, wg_ref, gb_ref, ng_ref, o_ref,
                lr_s, q_s, k_s2, st_s, *, tm, dk, dv):
    i = pl.program_id(1)
    h = pl.program_id(2)
    nc = tm // CHUNK

    @pl.when(h == 0)
    def _():
        lr_s[...] = _dot(x_ref[...], wlr_ref[...]).astype(BF16)

    @pl.when(i == 0)
    def _():
        st_s[h] = jnp.zeros((dk, dv), F32)

    @pl.when(lax.rem(h, 2) == 0)
    def _():
        q_s[...] = _dot(x_ref[...], wq_ref[...])
        k_s2[...] = _dot(x_ref[...], wk_ref[...])

    half = pl.ds(pl.multiple_of(lax.rem(h, 2) * dk, dk), dk)
    q = q_s[:, half] * (dk ** -0.5)
    k = k_s2[:, half]
    v = _dot(x_ref[...], wv_ref[...]).astype(BF16)

    z = _dot(lr_s[...], wg_ref[...]) + gb_ref[...]
    bcum = _log_sigmoid(z) / GATE_TAU
    row = lax.broadcasted_iota(jnp.int32, (tm, dk), 0) & (CHUNK - 1)
    s = 1
    while s < CHUNK:
        bcum = bcum + jnp.where(row >= s, pltpu.roll(bcum, s, axis=0), 0.0)
        s *= 2
    b3 = bcum.reshape(nc, CHUNK, dk)
    b_last = b3[:, CHUNK - 1:CHUNK, :]
    q_t = (q * jnp.exp(bcum)).astype(BF16)
    k_t = (k * jnp.exp(-bcum)).astype(BF16)
    k_s = k.reshape(nc, CHUNK, dk) * jnp.exp(b_last - b3)
    decay_t = jnp.exp(b_last.reshape(nc, dk)).T

    tri = lax.broadcasted_iota(jnp.int32, (CHUNK, CHUNK), 0) >= \
        lax.broadcasted_iota(jnp.int32, (CHUNK, CHUNK), 1)
    gate = _silu(_dot(x_ref[...], wgg_ref[...]))
    state = st_s[h]
    for c in range(nc):
        lo = c * CHUNK
        qc = q_t[lo:lo + CHUNK]
        vc = v[lo:lo + CHUNK]
        att = jnp.where(tri, _dot_nt(qc, k_t[lo:lo + CHUNK]), 0.0).astype(BF16)
        o = _dot(att, vc) + _dot(qc, state.astype(BF16))
        state = state * decay_t[:, c:c + 1] + _dot(k_s[c].T.astype(BF16), vc)
        o = o * lax.rsqrt(jnp.mean(o * o, axis=-1, keepdims=True) + LN_EPS) * ng_ref[...]
        o_ref[lo:lo + CHUNK, :] = (o * gate[lo:lo + CHUNK]).astype(o_ref.dtype)
    st_s[h] = state


def _gla_group(x_bf, w_all, w_tail, w_lr, w_gate, gate_b, norm_g, *, q_col, k_col, v_col, gate_col,
               batch, seq, tm):
    t, kdim = x_bf.shape
    heads, dk, dv = GLA_HEADS, GLA_DK, GLA_DV
    nt = seq // tm
    q_blk, k_blk, v_blk, g_blk = q_col // (2 * dk), k_col // (2 * dk), v_col // dv, gate_col // dv
    assert q_col % (2 * dk) == 0 and k_col % (2 * dk) == 0 and v_col % dv == 0 and gate_col % dv == 0
    assert heads % 2 == 0
    return pl.pallas_call(
        functools.partial(_gla_kernel, tm=tm, dk=dk, dv=dv),
        grid=(batch, nt, heads),
        in_specs=[pl.BlockSpec((tm, kdim), lambda bb, i, h: (bb * nt + i, 0)),
                  pl.BlockSpec((kdim, 2 * dk), lambda bb, i, h: (0, q_blk + h // 2)),
                  pl.BlockSpec((kdim, 2 * dk), lambda bb, i, h: (0, k_blk + h // 2)),
                  pl.BlockSpec((kdim, dv), lambda bb, i, h: (0, v_blk + h)),
                  pl.BlockSpec((kdim, dv), lambda bb, i, h: (0, g_blk + h)),
                  pl.BlockSpec((kdim, LANES), lambda bb, i, h: (0, 0)),
                  pl.BlockSpec((LANES, dk), lambda bb, i, h: (0, h)),
                  pl.BlockSpec((1, dk), lambda bb, i, h: (0, h)),
                  pl.BlockSpec((1, dv), lambda bb, i, h: (0, 0))],
        out_specs=pl.BlockSpec((tm, dv), lambda bb, i, h: (bb * nt + i, h)),
        out_shape=jax.ShapeDtypeStruct((t, heads * dv), BF16),
        scratch_shapes=[pltpu.VMEM((tm, LANES), BF16),
                        pltpu.VMEM((tm, 2 * dk), F32),
                        pltpu.VMEM((tm, 2 * dk), F32),
                        pltpu.VMEM((heads, dk, dv), F32)],
        compiler_params=pltpu.CompilerParams(
            dimension_semantics=("arbitrary", "arbitrary", "arbitrary"),
            vmem_limit_bytes=VMEM_LIMIT),
        name="gla_group",
    )(x_bf, w_all, w_all, w_all, w_tail, w_lr, w_gate, gate_b, norm_g)


def _out_kernel(yc_ref, yg_ref, ym_ref, w_ref, x_ref, g_ref, b_ref, o_ref, run_mean, run_m2,
                *, tm, tn, n_col, d):
    j = pl.program_id(1)
    col = pl.multiple_of(j * tn, tn)
    c_w = yc_ref.shape[1]
    g_w = yg_ref.shape[1]
    acc = _dot(yc_ref[...], w_ref[0:c_w, :])
    acc = acc + _dot(yg_ref[...], w_ref[c_w:c_w + g_w, :])
    acc = acc + _dot(ym_ref[...], w_ref[c_w + g_w:, :])
    r = acc + DN_ALPHA * x_ref[...]
    o_ref[:, pl.ds(col, tn)] = r

    blk_mean = jnp.mean(r, axis=-1, keepdims=True)
    dev = r - blk_mean
    blk_m2 = jnp.sum(dev * dev, axis=-1, keepdims=True)
    n_a = (j * tn).astype(F32)
    n_ab = n_a + tn
    delta = blk_mean - run_mean[...]
    run_m2[...] = jnp.where(j == 0, blk_m2, run_m2[...] + blk_m2 + delta * delta * (n_a * tn / n_ab))
    run_mean[...] = jnp.where(j == 0, blk_mean, run_mean[...] + delta * (tn / n_ab))

    @pl.when(j == n_col - 1)
    def _():
        run_m2[...] = lax.rsqrt(run_m2[...] / d + LN_EPS)

        def row_block(rb, carry):
            t0 = pl.multiple_of(rb * LN_ROWS, LN_ROWS)
            rows = pl.ds(t0, LN_ROWS)
            o_ref[rows, :] = (o_ref[rows, :] - run_mean[rows, :]) * run_m2[rows, :] * g_ref[...] \
                + b_ref[...]
            return carry

        lax.fori_loop(0, tm // LN_ROWS, row_block, 0)


def _out_proj(y_conv, y_gla, y_mem, w_out, x2, ln_g, ln_b, *, tm, tn):
    t, d = x2.shape
    n_col = d // tn
    kdim = w_out.shape[0]
    stat = pltpu.VMEM((tm, 1), F32)
    return pl.pallas_call(
        functools.partial(_out_kernel, tm=tm, tn=tn, n_col=n_col, d=d),
        grid=(t // tm, n_col),
        in_specs=[pl.BlockSpec((tm, y_conv.shape[1]), lambda i, j: (i, 0)),
                  pl.BlockSpec((tm, y_gla.shape[1]), lambda i, j: (i, 0)),
                  pl.BlockSpec((tm, y_mem.shape[1]), lambda i, j: (i, 0)),
                  pl.BlockSpec((kdim, tn), lambda i, j: (0, j)),
                  pl.BlockSpec((tm, tn), lambda i, j: (i, j)),
                  pl.BlockSpec((1, d), lambda i, j: (0, 0)),
                  pl.BlockSpec((1, d), lambda i, j: (0, 0))],
        out_specs=pl.BlockSpec((tm, d), lambda i, j: (i, 0)),
        out_shape=jax.ShapeDtypeStruct((t, d), F32),
        scratch_shapes=[stat, stat],
        compiler_params=pltpu.CompilerParams(
            dimension_semantics=("arbitrary", "arbitrary"), vmem_limit_bytes=VMEM_LIMIT),
        name="out_proj_ln",
    )(y_conv, y_gla, y_mem, w_out, x2, ln_g, ln_b)


_O_QK = 3 * CONV_CH
_O_K = _O_QK + GLA_KEY
_O_V = _O_K + GLA_KEY
_O_LR = _O_V + GLA_VAL
_O_GG = _O_LR + GATE_RANK
_O_MQ = _O_GG + GLA_VAL
_O_MG = _O_MQ + MEM_W


def kernel(x, mem, w_in, conv_dw, conv_dw_b, conv_ln_g, conv_ln_b, gla_w_gate, gla_gate_b,
           gla_norm_g, w_mem_kv, w_out, ln_g, ln_b):
    batch, seq, d = x.shape
    t = batch * seq
    x2 = x.reshape(t, d)
    for l in range(DEPTH):
        w_all = w_in[l].astype(BF16)
        w_tail = w_all[:, _O_GG:]
        w_lr = jnp.pad(w_all[:, _O_LR:_O_GG], ((0, 0), (0, LANES - GATE_RANK)))
        w_gate = jnp.pad(gla_w_gate[l].astype(BF16), ((0, LANES - GATE_RANK), (0, 0)))
        mem_kv = _memkv(mem.reshape(batch * N_MEM, d), w_mem_kv[l])
        mem_kv = mem_kv.reshape(batch, N_MEM, 2 * MEM_W)
        y_mem, x_bf = _mem_group(x2, w_tail, mem_kv, q_col=_O_MQ - _O_GG, gate_col=_O_MG - _O_GG,
                                 batch=batch, seq=seq, tm=512)
        y_conv = _conv_group(x_bf, w_all, conv_dw[l], conv_dw_b[l][None], conv_ln_g[l][None],
                             conv_ln_b[l][None], batch=batch, seq=seq, tm=512)
        y_gla = _gla_group(x_bf, w_all, w_tail, w_lr, w_gate, gla_gate_b[l][None],
                           gla_norm_g[l][None], q_col=_O_QK, k_col=_O_K, v_col=_O_V, gate_col=0,
                           batch=batch, seq=seq, tm=512)
        x2 = _out_proj(y_conv, y_gla, y_mem, w_out[l].astype(BF16), x2, ln_g[l][None],
                       ln_b[l][None], tm=512, tn=1024)
    return x2.reshape(batch, seq, d)
```

```python
import functools

import jax
import jax.numpy as jnp
from jax import lax
from jax.experimental import pallas as pl
from jax.experimental.pallas import tpu as pltpu

D_MODEL = 4096
N_MEM = 256
MEM_HEADS = 4
MEM_W = D_MODEL // 4
MEM_HD = MEM_W // MEM_HEADS
CONV_CH = D_MODEL // 4
CONV_WIDTH = 31
GLA_HEADS = 4
GLA_VAL = D_MODEL // 2
GLA_DV = GLA_VAL // GLA_HEADS
GLA_DK = GLA_DV // 2
GLA_KEY = GLA_HEADS * GLA_DK
GATE_RANK = 16
GATE_TAU = 16.0
CHUNK = 64
LN_EPS = 1e-5
DEPTH = 1
DN_ALPHA = (2 * DEPTH) ** 0.25

LANES = 128
SUBLANES = 8
VMEM_LIMIT = 56 * 1024 * 1024

HALO = 32
CONV_ROWS = 64
LN_ROWS = 64
BF16 = jnp.bfloat16
F32 = jnp.float32


def _dot(a, b):
    return jnp.dot(a, b, preferred_element_type=F32)


def _dot_nt(a, b):
    return lax.dot_general(a, b, (((1,), (1,)), ((), ())), preferred_element_type=F32)


def _sigmoid(x):
    return 1.0 / (1.0 + jnp.exp(-x))


def _silu(x):
    return x * _sigmoid(x)


def _prep_kernel(w_ref, all_ref, tail_ref, *, tail_col):
    w = w_ref[...]
    all_ref[...] = w.astype(BF16)
    tail_ref[...] = w[:, tail_col:].astype(BF16)


def _prep_w_in(w, tail_col, rows=256):
    k, n = w.shape
    return pl.pallas_call(
        functools.partial(_prep_kernel, tail_col=tail_col),
        grid=(k // rows,),
        in_specs=[pl.BlockSpec((rows, n), lambda i: (i, 0))],
        out_specs=[pl.BlockSpec((rows, n), lambda i: (i, 0)),
                   pl.BlockSpec((rows, n - tail_col), lambda i: (i, 0))],
        out_shape=[jax.ShapeDtypeStruct((k, n), BF16),
                   jax.ShapeDtypeStruct((k, n - tail_col), BF16)],
        compiler_params=pltpu.CompilerParams(
            dimension_semantics=("arbitrary",), vmem_limit_bytes=VMEM_LIMIT),
        name="prep_w_in",
    )(w)


def _memkv_kernel(m_ref, w_ref, o_ref):
    o_ref[...] = _dot(m_ref[...].astype(BF16), w_ref[...].astype(BF16)).astype(o_ref.dtype)


def _memkv(mem2, w, tn=512):
    m, k = mem2.shape
    n = w.shape[1]
    return pl.pallas_call(
        _memkv_kernel,
        grid=(n // tn,),
        in_specs=[pl.BlockSpec((m, k), lambda j: (0, 0)),
                  pl.BlockSpec((k, tn), lambda j: (0, j))],
        out_specs=pl.BlockSpec((m, tn), lambda j: (0, j)),
        out_shape=jax.ShapeDtypeStruct((m, n), BF16),
        compiler_params=pltpu.CompilerParams(
            dimension_semantics=("arbitrary",), vmem_limit_bytes=VMEM_LIMIT),
        name="memkv_proj",
    )(mem2, w)


def _mem_kernel(x_ref, wq_ref, wg_ref, k_ref, v_ref, o_ref, xb_ref, *, hd):
    @pl.when(pl.program_id(2) == 0)
    def _():
        xb_ref[...] = x_ref[...].astype(BF16)

    q2 = _dot(xb_ref[...], wq_ref[...]).astype(BF16)
    g2 = _dot(xb_ref[...], wg_ref[...])
    for hh in range(2):
        cols = slice(hh * hd, (hh + 1) * hd)
        s = _dot_nt(q2[:, cols], k_ref[0, :, cols]) * (hd ** -0.5)
        s = s - jnp.max(s, axis=-1, keepdims=True)
        e = jnp.exp(s)
        p = (e / jnp.sum(e, axis=-1, keepdims=True)).astype(BF16)
        o = _dot(p, v_ref[0, :, cols])
        o_ref[:, cols] = (o * _silu(g2[:, cols])).astype(o_ref.dtype)


def _mem_group(x2, w_tail, mem_kv, *, q_col, gate_col, batch, seq, tm):
    t, kdim = x2.shape
    heads, hd = MEM_HEADS, MEM_HD
    nt = seq // tm
    n_mem = mem_kv.shape[1]
    q_blk, g_blk = q_col // (2 * hd), gate_col // (2 * hd)
    assert q_col % (2 * hd) == 0 and gate_col % (2 * hd) == 0 and heads % 2 == 0
    pairs = heads // 2
    return pl.pallas_call(
        functools.partial(_mem_kernel, hd=hd),
        grid=(batch, nt, pairs),
        in_specs=[pl.BlockSpec((tm, kdim), lambda bb, i, p: (bb * nt + i, 0)),
                  pl.BlockSpec((kdim, 2 * hd), lambda bb, i, p: (0, q_blk + p)),
                  pl.BlockSpec((kdim, 2 * hd), lambda bb, i, p: (0, g_blk + p)),
                  pl.BlockSpec((1, n_mem, 2 * hd), lambda bb, i, p: (bb, 0, p)),
                  pl.BlockSpec((1, n_mem, 2 * hd), lambda bb, i, p: (bb, 0, pairs + p))],
        out_specs=[pl.BlockSpec((tm, 2 * hd), lambda bb, i, p: (bb * nt + i, p)),
                   pl.BlockSpec((tm, kdim), lambda bb, i, p: (bb * nt + i, 0))],
        out_shape=[jax.ShapeDtypeStruct((t, heads * hd), BF16),
                   jax.ShapeDtypeStruct((t, kdim), BF16)],
        compiler_params=pltpu.CompilerParams(
            dimension_semantics=("arbitrary", "arbitrary", "arbitrary"),
            vmem_limit_bytes=VMEM_LIMIT),
        name="mem_group",
    )(x2, w_tail, w_tail, mem_kv, mem_kv)


def _conv_kernel(x_ref, w_ref, dw_ref, dwb_ref, g_ref, b_ref, o_ref, a_s, h_s, *, tm, ch):
    i = pl.program_id(1)
    j = pl.program_id(2)

    @pl.when(j == 0)
    def _():
        a_s[...] = _dot(x_ref[...], w_ref[...])

    @pl.when(j == 1)
    def _():
        tail = h_s[tm:tm + HALO, :]
        h_s[0:HALO, :] = jnp.where(i == 0, 0.0, tail)
        h_s[HALO:HALO + tm, :] = a_s[...] * _sigmoid(_dot(x_ref[...], w_ref[...]))

    @pl.when(j == 2)
    def _():
        a_s[...] = _silu(_dot(x_ref[...], w_ref[...]))
        n_win = CONV_ROWS + HALO

        def row_block(rb, carry):
            t0 = pl.multiple_of(rb * CONV_ROWS, CONV_ROWS)
            parts = []
            for c0 in range(0, ch, LANES):
                win = h_s[pl.ds(t0, n_win), c0:c0 + LANES]
                acc = jnp.zeros((CONV_ROWS, LANES), F32)
                for r in range(SUBLANES):
                    shifted = win if r == 0 else pltpu.roll(win, n_win - r, axis=0)
                    for q in range(HALO // SUBLANES + 1):
                        tap = q * SUBLANES + r - (HALO - CONV_WIDTH + 1)
                        if 0 <= tap < CONV_WIDTH:
                            acc = acc + dw_ref[tap:tap + 1, c0:c0 + LANES] * \
                                shifted[q * SUBLANES:q * SUBLANES + CONV_ROWS, :]
                parts.append(acc)
            hc = jnp.concatenate(parts, axis=1) + dwb_ref[...]
            mu = jnp.mean(hc, axis=-1, keepdims=True)
            d = hc - mu
            var = jnp.mean(d * d, axis=-1, keepdims=True)
            y = d * lax.rsqrt(var + LN_EPS) * g_ref[...] + b_ref[...]
            y = _silu(y) * a_s[pl.ds(t0, CONV_ROWS), :]
            o_ref[pl.ds(t0, CONV_ROWS), :] = y.astype(o_ref.dtype)
            return carry

        lax.fori_loop(0, tm // CONV_ROWS, row_block, 0)


def _conv_group(x_bf, w_conv, dw, dwb, g, b, *, batch, seq, tm):
    t, k = x_bf.shape
    ch = dw.shape[1]
    nt = seq // tm
    return pl.pallas_call(
        functools.partial(_conv_kernel, tm=tm, ch=ch),
        grid=(batch, nt, 3),
        in_specs=[pl.BlockSpec((tm, k), lambda bb, i, j: (bb * nt + i, 0)),
                  pl.BlockSpec((k, ch), lambda bb, i, j: (0, j)),
                  pl.BlockSpec((CONV_WIDTH, ch), lambda bb, i, j: (0, 0)),
                  pl.BlockSpec((1, ch), lambda bb, i, j: (0, 0)),
                  pl.BlockSpec((1, ch), lambda bb, i, j: (0, 0)),
                  pl.BlockSpec((1, ch), lambda bb, i, j: (0, 0))],
        out_specs=pl.BlockSpec((tm, ch), lambda bb, i, j: (bb * nt + i, 0)),
        out_shape=jax.ShapeDtypeStruct((t, ch), BF16),
        scratch_shapes=[pltpu.VMEM((tm, ch), F32),
                        pltpu.VMEM((tm + HALO, ch), F32)],
        compiler_params=pltpu.CompilerParams(
            dimension_semantics=("arbitrary", "arbitrary", "arbitrary"),
            vmem_limit_bytes=VMEM_LIMIT),
        name="conv_group",
    )(x_bf, w_conv, dw, dwb, g, b)


def _log_sigmoid(z):
    return jnp.minimum(z, 0.0) - jnp.log(1.0 + jnp.exp(-jnp.abs(z)))


def _gla_kernel(x_ref, wq_ref, wk_ref, wv_ref, wgg_ref, wlr_ref, wg_ref, gb_ref, ng_ref, o_ref,
                lr_s, q_s, k_s2, st_s, *, tm, dk, dv):
    i = pl.program_id(1)
    h = pl.program_id(2)
    nc = tm // CHUNK

    @pl.when(h == 0)
    def _():
        lr_s[...] = _dot(x_ref[...], wlr_ref[...]).astype(BF16)

    @pl.when(i == 0)
    def _():
        st_s[h] = jnp.zeros((dk, dv), F32)

    @pl.when(lax.rem(h, 2) == 0)
    def _():
        q_s[...] = _dot(x_ref[...], wq_ref[...])
        k_s2[...] = _dot(x_ref[...], wk_ref[...])

    half = pl.ds(pl.multiple_of(lax.rem(h, 2) * dk, dk), dk)
    q = q_s[:, half] * (dk ** -0.5)
    k = k_s2[:, half]
    v = _dot(x_ref[...], wv_ref[...]).astype(BF16)

    z = _dot(lr_s[...], wg_ref[...]) + gb_ref[...]
    bcum = _log_sigmoid(z) / GATE_TAU
    row = lax.broadcasted_iota(jnp.int32, (tm, dk), 0) & (CHUNK - 1)
    s = 1
    while s < CHUNK:
        bcum = bcum + jnp.where(row >= s, pltpu.roll(bcum, s, axis=0), 0.0)
        s *= 2
    b3 = bcum.reshape(nc, CHUNK, dk)
    b_last = b3[:, CHUNK - 1:CHUNK, :]
    q_t = (q * jnp.exp(bcum)).astype(BF16)
    k_t = (k * jnp.exp(-bcum)).astype(BF16)
    k_s = k.reshape(nc, CHUNK, dk) * jnp.exp(b_last - b3)
    decay_t = jnp.exp(b_last.reshape(nc, dk)).T

    tri = lax.broadcasted_iota(jnp.int32, (CHUNK, CHUNK), 0) >= \
        lax.broadcasted_iota(jnp.int32, (CHUNK, CHUNK), 1)
    gate = _silu(_dot(x_ref[...], wgg_ref[...]))
    state = st_s[h]
    for c in range(nc):
        lo = c * CHUNK
        qc = q_t[lo:lo + CHUNK]
        vc = v[lo:lo + CHUNK]
        att = jnp.where(tri, _dot_nt(qc, k_t[lo:lo + CHUNK]), 0.0).astype(BF16)
        o = _dot(att, vc) + _dot(qc, state.astype(BF16))
        state = state * decay_t[:, c:c + 1] + _dot(k_s[c].T.astype(BF16), vc)
        o = o * lax.rsqrt(jnp.mean(o * o, axis=-1, keepdims=True) + LN_EPS) * ng_ref[...]
        o_ref[lo:lo + CHUNK, :] = (o * gate[lo:lo + CHUNK]).astype(o_ref.dtype)
    st_s[h] = state


def _gla_group(x_bf, w_all, w_tail, w_lr, w_gate, gate_b, norm_g, *, q_col, k_col, v_col, gate_col,
               batch, seq, tm):
    t, kdim = x_bf.shape
    heads, dk, dv = GLA_HEADS, GLA_DK, GLA_DV
    nt = seq // tm
    q_blk, k_blk, v_blk, g_blk = q_col // (2 * dk), k_col // (2 * dk), v_col // dv, gate_col // dv
    assert q_col % (2 * dk) == 0 and k_col % (2 * dk) == 0 and v_col % dv == 0 and gate_col % dv == 0
    assert heads % 2 == 0
    return pl.pallas_call(
        functools.partial(_gla_kernel, tm=tm, dk=dk, dv=dv),
        grid=(batch, nt, heads),
        in_specs=[pl.BlockSpec((tm, kdim), lambda bb, i, h: (bb * nt + i, 0)),
                  pl.BlockSpec((kdim, 2 * dk), lambda bb, i, h: (0, q_blk + h // 2)),
                  pl.BlockSpec((kdim, 2 * dk), lambda bb, i, h: (0, k_blk + h // 2)),
                  pl.BlockSpec((kdim, dv), lambda bb, i, h: (0, v_blk + h)),
                  pl.BlockSpec((kdim, dv), lambda bb, i, h: (0, g_blk + h)),
                  pl.BlockSpec((kdim, LANES), lambda bb, i, h: (0, 0)),
                  pl.BlockSpec((LANES, dk), lambda bb, i, h: (0, h)),
                  pl.BlockSpec((1, dk), lambda bb, i, h: (0, h)),
                  pl.BlockSpec((1, dv), lambda bb, i, h: (0, 0))],
        out_specs=pl.BlockSpec((tm, dv), lambda bb, i, h: (bb * nt + i, h)),
        out_shape=jax.ShapeDtypeStruct((t, heads * dv), BF16),
        scratch_shapes=[pltpu.VMEM((tm, LANES), BF16),
                        pltpu.VMEM((tm, 2 * dk), F32),
                        pltpu.VMEM((tm, 2 * dk), F32),
                        pltpu.VMEM((heads, dk, dv), F32)],
        compiler_params=pltpu.CompilerParams(
            dimension_semantics=("arbitrary", "arbitrary", "arbitrary"),
            vmem_limit_bytes=VMEM_LIMIT),
        name="gla_group",
    )(x_bf, w_all, w_all, w_all, w_tail, w_lr, w_gate, gate_b, norm_g)


def _out_kernel(yc_ref, yg_ref, ym_ref, w_ref, x_ref, g_ref, b_ref, o_ref, run_mean, run_m2,
                *, tm, tn, n_col, d):
    j = pl.program_id(1)
    col = pl.multiple_of(j * tn, tn)
    c_w = yc_ref.shape[1]
    g_w = yg_ref.shape[1]
    acc = _dot(yc_ref[...], w_ref[0:c_w, :])
    acc = acc + _dot(yg_ref[...], w_ref[c_w:c_w + g_w, :])
    acc = acc + _dot(ym_ref[...], w_ref[c_w + g_w:, :])
    r = acc + DN_ALPHA * x_ref[...]
    o_ref[:, pl.ds(col, tn)] = r

    blk_mean = jnp.mean(r, axis=-1, keepdims=True)
    dev = r - blk_mean
    blk_m2 = jnp.sum(dev * dev, axis=-1, keepdims=True)
    n_a = (j * tn).astype(F32)
    n_ab = n_a + tn
    delta = blk_mean - run_mean[...]
    run_m2[...] = jnp.where(j == 0, blk_m2, run_m2[...] + blk_m2 + delta * delta * (n_a * tn / n_ab))
    run_mean[...] = jnp.where(j == 0, blk_mean, run_mean[...] + delta * (tn / n_ab))

    @pl.when(j == n_col - 1)
    def _():
        run_m2[...] = lax.rsqrt(run_m2[...] / d + LN_EPS)

        def row_block(rb, carry):
            t0 = pl.multiple_of(rb * LN_ROWS, LN_ROWS)
            rows = pl.ds(t0, LN_ROWS)
            o_ref[rows, :] = (o_ref[rows, :] - run_mean[rows, :]) * run_m2[rows, :] * g_ref[...] \
                + b_ref[...]
            return carry

        lax.fori_loop(0, tm // LN_ROWS, row_block, 0)


def _out_proj(y_conv, y_gla, y_mem, w_out, x2, ln_g, ln_b, *, tm, tn):
    t, d = x2.shape
    n_col = d // tn
    kdim = w_out.shape[0]
    stat = pltpu.VMEM((tm, 1), F32)
    return pl.pallas_call(
        functools.partial(_out_kernel, tm=tm, tn=tn, n_col=n_col, d=d),
        grid=(t // tm, n_col),
        in_specs=[pl.BlockSpec((tm, y_conv.shape[1]), lambda i, j: (i, 0)),
                  pl.BlockSpec((tm, y_gla.shape[1]), lambda i, j: (i, 0)),
                  pl.BlockSpec((tm, y_mem.shape[1]), lambda i, j: (i, 0)),
                  pl.BlockSpec((kdim, tn), lambda i, j: (0, j)),
                  pl.BlockSpec((tm, tn), lambda i, j: (i, j)),
                  pl.BlockSpec((1, d), lambda i, j: (0, 0)),
                  pl.BlockSpec((1, d), lambda i, j: (0, 0))],
        out_specs=pl.BlockSpec((tm, d), lambda i, j: (i, 0)),
        out_shape=jax.ShapeDtypeStruct((t, d), F32),
        scratch_shapes=[stat, stat],
        compiler_params=pltpu.CompilerParams(
            dimension_semantics=("arbitrary", "arbitrary"), vmem_limit_bytes=VMEM_LIMIT),
        name="out_proj_ln",
    )(y_conv, y_gla, y_mem, w_out, x2, ln_g, ln_b)


_O_QK = 3 * CONV_CH
_O_K = _O_QK + GLA_KEY
_O_V = _O_K + GLA_KEY
_O_LR = _O_V + GLA_VAL
_O_GG = _O_LR + GATE_RANK
_O_MQ = _O_GG + GLA_VAL
_O_MG = _O_MQ + MEM_W


def kernel(x, mem, w_in, conv_dw, conv_dw_b, conv_ln_g, conv_ln_b, gla_w_gate, gla_gate_b,
           gla_norm_g, w_mem_kv, w_out, ln_g, ln_b):
    batch, seq, d = x.shape
    t = batch * seq
    x2 = x.reshape(t, d)
    for l in range(DEPTH):
        w_all, w_tail = _prep_w_in(w_in[l], _O_GG)
        w_lr = jnp.pad(w_all[:, _O_LR:_O_GG], ((0, 0), (0, LANES - GATE_RANK)))
        w_gate = jnp.pad(gla_w_gate[l].astype(BF16), ((0, LANES - GATE_RANK), (0, 0)))
        mem_kv = _memkv(mem.reshape(batch * N_MEM, d), w_mem_kv[l])
        mem_kv = mem_kv.reshape(batch, N_MEM, 2 * MEM_W)
        y_mem, x_bf = _mem_group(x2, w_tail, mem_kv, q_col=_O_MQ - _O_GG, gate_col=_O_MG - _O_GG,
                                 batch=batch, seq=seq, tm=512)
        y_conv = _conv_group(x_bf, w_all, conv_dw[l], conv_dw_b[l][None], conv_ln_g[l][None],
                             conv_ln_b[l][None], batch=batch, seq=seq, tm=512)
        y_gla = _gla_group(x_bf, w_all, w_tail, w_lr, w_gate, gla_gate_b[l][None],
                           gla_norm_g[l][None], q_col=_O_QK, k_col=_O_K, v_col=_O_V, gate_col=0,
                           batch=batch, seq=seq, tm=512)
        x2 = _out_proj(y_conv, y_gla, y_mem, w_out[l].astype(BF16), x2, ln_g[l][None],
                       ln_b[l][None], tm=512, tn=1024)
    return x2.reshape(batch, seq, d)
```

```python
import functools

import jax
import jax.numpy as jnp
from jax import lax
from jax.experimental import pallas as pl
from jax.experimental.pallas import tpu as pltpu

D_MODEL = 4096
N_MEM = 256
MEM_HEADS = 4
MEM_W = D_MODEL // 4
MEM_HD = MEM_W // MEM_HEADS
CONV_CH = D_MODEL // 4
CONV_WIDTH = 31
GLA_HEADS = 4
GLA_VAL = D_MODEL // 2
GLA_DV = GLA_VAL // GLA_HEADS
GLA_DK = GLA_DV // 2
GLA_KEY = GLA_HEADS * GLA_DK
GATE_RANK = 16
GATE_TAU = 16.0
CHUNK = 64
LN_EPS = 1e-5
DEPTH = 1
DN_ALPHA = (2 * DEPTH) ** 0.25

LANES = 128
SUBLANES = 8
VMEM_LIMIT = 56 * 1024 * 1024

HALO = 32
CONV_ROWS = 64
LN_ROWS = 64
BF16 = jnp.bfloat16
F32 = jnp.float32


def _dot(a, b):
    return jnp.dot(a, b, preferred_element_type=F32)


def _dot_nt(a, b):
    return lax.dot_general(a, b, (((1,), (1,)), ((), ())), preferred_element_type=F32)


def _sigmoid(x):
    return 1.0 / (1.0 + jnp.exp(-x))


def _silu(x):
    return x * _sigmoid(x)


def _memkv_kernel(m_ref, w_ref, o_ref):
    o_ref[...] = _dot(m_ref[...].astype(BF16), w_ref[...].astype(BF16)).astype(o_ref.dtype)


def _memkv(mem2, w, tn=512):
    m, k = mem2.shape
    n = w.shape[1]
    return pl.pallas_call(
        _memkv_kernel,
        grid=(n // tn,),
        in_specs=[pl.BlockSpec((m, k), lambda j: (0, 0)),
                  pl.BlockSpec((k, tn), lambda j: (0, j))],
        out_specs=pl.BlockSpec((m, tn), lambda j: (0, j)),
        out_shape=jax.ShapeDtypeStruct((m, n), BF16),
        compiler_params=pltpu.CompilerParams(
            dimension_semantics=("arbitrary",), vmem_limit_bytes=VMEM_LIMIT),
        name="memkv_proj",
    )(mem2, w)


def _mem_kernel(x_ref, wq_ref, wg_ref, k_ref, v_ref, o_ref, xb_ref, *, hd):
    @pl.when(pl.program_id(2) == 0)
    def _():
        xb_ref[...] = x_ref[...].astype(BF16)

    q2 = _dot_nt(xb_ref[...], wq_ref[...]).astype(BF16)
    g2 = _dot_nt(xb_ref[...], wg_ref[...])
    for hh in range(2):
        cols = slice(hh * hd, (hh + 1) * hd)
        s = _dot_nt(q2[:, cols], k_ref[0, :, cols]) * (hd ** -0.5)
        s = s - jnp.max(s, axis=-1, keepdims=True)
        e = jnp.exp(s)
        p = (e / jnp.sum(e, axis=-1, keepdims=True)).astype(BF16)
        o = _dot(p, v_ref[0, :, cols])
        o_ref[:, cols] = (o * _silu(g2[:, cols])).astype(o_ref.dtype)


def _mem_group(x2, w_tail, mem_kv, *, q_col, gate_col, batch, seq, tm):
    t, kdim = x2.shape
    heads, hd = MEM_HEADS, MEM_HD
    nt = seq // tm
    n_mem = mem_kv.shape[1]
    q_blk, g_blk = q_col // (2 * hd), gate_col // (2 * hd)
    assert q_col % (2 * hd) == 0 and gate_col % (2 * hd) == 0 and heads % 2 == 0
    pairs = heads // 2
    return pl.pallas_call(
        functools.partial(_mem_kernel, hd=hd),
        grid=(batch, nt, pairs),
        in_specs=[pl.BlockSpec((tm, kdim), lambda bb, i, p: (bb * nt + i, 0)),
                  pl.BlockSpec((2 * hd, kdim), lambda bb, i, p: (q_blk + p, 0)),
                  pl.BlockSpec((2 * hd, kdim), lambda bb, i, p: (g_blk + p, 0)),
                  pl.BlockSpec((1, n_mem, 2 * hd), lambda bb, i, p: (bb, 0, p)),
                  pl.BlockSpec((1, n_mem, 2 * hd), lambda bb, i, p: (bb, 0, pairs + p))],
        out_specs=[pl.BlockSpec((tm, 2 * hd), lambda bb, i, p: (bb * nt + i, p)),
                   pl.BlockSpec((tm, kdim), lambda bb, i, p: (bb * nt + i, 0))],
        out_shape=[jax.ShapeDtypeStruct((t, heads * hd), BF16),
                   jax.ShapeDtypeStruct((t, kdim), BF16)],
        compiler_params=pltpu.CompilerParams(
            dimension_semantics=("arbitrary", "arbitrary", "arbitrary"),
            vmem_limit_bytes=VMEM_LIMIT),
        name="mem_group",
    )(x2, w_tail, w_tail, mem_kv, mem_kv)


def _conv_kernel(x_ref, w_ref, dw_ref, dwb_ref, g_ref, b_ref, o_ref, a_s, h_s, *, tm, ch):
    i = pl.program_id(1)
    j = pl.program_id(2)

    @pl.when(j == 0)
    def _():
        a_s[...] = _dot_nt(x_ref[...], w_ref[...])

    @pl.when(j == 1)
    def _():
        tail = h_s[tm:tm + HALO, :]
        h_s[0:HALO, :] = jnp.where(i == 0, 0.0, tail)
        h_s[HALO:HALO + tm, :] = a_s[...] * _sigmoid(_dot_nt(x_ref[...], w_ref[...]))

    @pl.when(j == 2)
    def _():
        a_s[...] = _silu(_dot_nt(x_ref[...], w_ref[...]))
        n_win = CONV_ROWS + HALO

        def row_block(rb, carry):
            t0 = pl.multiple_of(rb * CONV_ROWS, CONV_ROWS)
            parts = []
            for c0 in range(0, ch, LANES):
                win = h_s[pl.ds(t0, n_win), c0:c0 + LANES]
                acc = jnp.zeros((CONV_ROWS, LANES), F32)
                for r in range(SUBLANES):
                    shifted = win if r == 0 else pltpu.roll(win, n_win - r, axis=0)
                    for q in range(HALO // SUBLANES + 1):
                        tap = q * SUBLANES + r - (HALO - CONV_WIDTH + 1)
                        if 0 <= tap < CONV_WIDTH:
                            acc = acc + dw_ref[tap:tap + 1, c0:c0 + LANES] * \
                                shifted[q * SUBLANES:q * SUBLANES + CONV_ROWS, :]
                parts.append(acc)
            hc = jnp.concatenate(parts, axis=1) + dwb_ref[...]
            mu = jnp.mean(hc, axis=-1, keepdims=True)
            d = hc - mu
            var = jnp.mean(d * d, axis=-1, keepdims=True)
            y = d * lax.rsqrt(var + LN_EPS) * g_ref[...] + b_ref[...]
            y = _silu(y) * a_s[pl.ds(t0, CONV_ROWS), :]
            o_ref[pl.ds(t0, CONV_ROWS), :] = y.astype(o_ref.dtype)
            return carry

        lax.fori_loop(0, tm // CONV_ROWS, row_block, 0)


def _conv_group(x_bf, w_conv, dw, dwb, g, b, *, batch, seq, tm):
    t, k = x_bf.shape
    ch = dw.shape[1]
    nt = seq // tm
    return pl.pallas_call(
        functools.partial(_conv_kernel, tm=tm, ch=ch),
        grid=(batch, nt, 3),
        in_specs=[pl.BlockSpec((tm, k), lambda bb, i, j: (bb * nt + i, 0)),
                  pl.BlockSpec((ch, k), lambda bb, i, j: (j, 0)),
                  pl.BlockSpec((CONV_WIDTH, ch), lambda bb, i, j: (0, 0)),
                  pl.BlockSpec((1, ch), lambda bb, i, j: (0, 0)),
                  pl.BlockSpec((1, ch), lambda bb, i, j: (0, 0)),
                  pl.BlockSpec((1, ch), lambda bb, i, j: (0, 0))],
        out_specs=pl.BlockSpec((tm, ch), lambda bb, i, j: (bb * nt + i, 0)),
        out_shape=jax.ShapeDtypeStruct((t, ch), BF16),
        scratch_shapes=[pltpu.VMEM((tm, ch), F32),
                        pltpu.VMEM((tm + HALO, ch), F32)],
        compiler_params=pltpu.CompilerParams(
            dimension_semantics=("arbitrary", "arbitrary", "arbitrary"),
            vmem_limit_bytes=VMEM_LIMIT),
        name="conv_group",
    )(x_bf, w_conv, dw, dwb, g, b)


def _log_sigmoid(z):
    return jnp.minimum(z, 0.0) - jnp.log(1.0 + jnp.exp(-jnp.abs(z)))


def _gla_kernel(x_ref, wq_ref, wk_ref, wv_ref, wgg_ref, wlr_ref, wg_ref, gb_ref, ng_ref, o_ref,
                lr_s, q_s, k_s2, st_s, *, tm, dk, dv):
    i = pl.program_id(1)
    h = pl.program_id(2)
    nc = tm // CHUNK

    @pl.when(h == 0)
    def _():
        lr_s[...] = _dot_nt(x_ref[...], wlr_ref[...]).astype(BF16)

    @pl.when(i == 0)
    def _():
        st_s[h] = jnp.zeros((dk, dv), F32)

    @pl.when(lax.rem(h, 2) == 0)
    def _():
        q_s[...] = _dot_nt(x_ref[...], wq_ref[...])
        k_s2[...] = _dot_nt(x_ref[...], wk_ref[...])

    half = pl.ds(pl.multiple_of(lax.rem(h, 2) * dk, dk), dk)
    q = q_s[:, half] * (dk ** -0.5)
    k = k_s2[:, half]
    v = _dot_nt(x_ref[...], wv_ref[...]).astype(BF16)

    z = _dot(lr_s[...], wg_ref[...]) + gb_ref[...]
    bcum = _log_sigmoid(z) / GATE_TAU
    row = lax.broadcasted_iota(jnp.int32, (tm, dk), 0) & (CHUNK - 1)
    s = 1
    while s < CHUNK:
        bcum = bcum + jnp.where(row >= s, pltpu.roll(bcum, s, axis=0), 0.0)
        s *= 2
    b3 = bcum.reshape(nc, CHUNK, dk)
    b_last = b3[:, CHUNK - 1:CHUNK, :]
    q_t = (q * jnp.exp(bcum)).astype(BF16)
    k_t = (k * jnp.exp(-bcum)).astype(BF16)
    k_s = k.reshape(nc, CHUNK, dk) * jnp.exp(b_last - b3)
    decay_t = jnp.exp(b_last.reshape(nc, dk)).T

    tri = lax.broadcasted_iota(jnp.int32, (CHUNK, CHUNK), 0) >= \
        lax.broadcasted_iota(jnp.int32, (CHUNK, CHUNK), 1)
    gate = _silu(_dot_nt(x_ref[...], wgg_ref[...]))
    state = st_s[h]
    for c in range(nc):
        lo = c * CHUNK
        qc = q_t[lo:lo + CHUNK]
        vc = v[lo:lo + CHUNK]
        att = jnp.where(tri, _dot_nt(qc, k_t[lo:lo + CHUNK]), 0.0).astype(BF16)
        o = _dot(att, vc) + _dot(qc, state.astype(BF16))
        state = state * decay_t[:, c:c + 1] + _dot(k_s[c].T.astype(BF16), vc)
        o = o * lax.rsqrt(jnp.mean(o * o, axis=-1, keepdims=True) + LN_EPS) * ng_ref[...]
        o_ref[lo:lo + CHUNK, :] = (o * gate[lo:lo + CHUNK]).astype(o_ref.dtype)
    st_s[h] = state


def _gla_group(x_bf, w_all, w_tail, w_lr, w_gate, gate_b, norm_g, *, q_col, k_col, v_col, gate_col,
               batch, seq, tm):
    t, kdim = x_bf.shape
    heads, dk, dv = GLA_HEADS, GLA_DK, GLA_DV
    nt = seq // tm
    q_blk, k_blk, v_blk, g_blk = q_col // (2 * dk), k_col // (2 * dk), v_col // dv, gate_col // dv
    assert q_col % (2 * dk) == 0 and k_col % (2 * dk) == 0 and v_col % dv == 0 and gate_col % dv == 0
    assert heads % 2 == 0
    return pl.pallas_call(
        functools.partial(_gla_kernel, tm=tm, dk=dk, dv=dv),
        grid=(batch, nt, heads),
        in_specs=[pl.BlockSpec((tm, kdim), lambda bb, i, h: (bb * nt + i, 0)),
                  pl.BlockSpec((2 * dk, kdim), lambda bb, i, h: (q_blk + h // 2, 0)),
                  pl.BlockSpec((2 * dk, kdim), lambda bb, i, h: (k_blk + h // 2, 0)),
                  pl.BlockSpec((dv, kdim), lambda bb, i, h: (v_blk + h, 0)),
                  pl.BlockSpec((dv, kdim), lambda bb, i, h: (g_blk + h, 0)),
                  pl.BlockSpec((LANES, kdim), lambda bb, i, h: (0, 0)),
                  pl.BlockSpec((LANES, dk), lambda bb, i, h: (0, h)),
                  pl.BlockSpec((1, dk), lambda bb, i, h: (0, h)),
                  pl.BlockSpec((1, dv), lambda bb, i, h: (0, 0))],
        out_specs=pl.BlockSpec((tm, dv), lambda bb, i, h: (bb * nt + i, h)),
        out_shape=jax.ShapeDtypeStruct((t, heads * dv), BF16),
        scratch_shapes=[pltpu.VMEM((tm, LANES), BF16),
                        pltpu.VMEM((tm, 2 * dk), F32),
                        pltpu.VMEM((tm, 2 * dk), F32),
                        pltpu.VMEM((heads, dk, dv), F32)],
        compiler_params=pltpu.CompilerParams(
            dimension_semantics=("arbitrary", "arbitrary", "arbitrary"),
            vmem_limit_bytes=VMEM_LIMIT),
        name="gla_group",
    )(x_bf, w_all, w_all, w_all, w_tail, w_lr, w_gate, gate_b, norm_g)


def _out_kernel(yc_ref, yg_ref, ym_ref, w_ref, x_ref, g_ref, b_ref, o_ref, run_mean, run_m2,
                *, tm, tn, n_col, d):
    j = pl.program_id(1)
    col = pl.multiple_of(j * tn, tn)
    c_w = yc_ref.shape[1]
    g_w = yg_ref.shape[1]
    acc = _dot(yc_ref[...], w_ref[0:c_w, :])
    acc = acc + _dot(yg_ref[...], w_ref[c_w:c_w + g_w, :])
    acc = acc + _dot(ym_ref[...], w_ref[c_w + g_w:, :])
    r = acc + DN_ALPHA * x_ref[...]
    o_ref[:, pl.ds(col, tn)] = r

    blk_mean = jnp.mean(r, axis=-1, keepdims=True)
    dev = r - blk_mean
    blk_m2 = jnp.sum(dev * dev, axis=-1, keepdims=True)
    n_a = (j * tn).astype(F32)
    n_ab = n_a + tn
    delta = blk_mean - run_mean[...]
    run_m2[...] = jnp.where(j == 0, blk_m2, run_m2[...] + blk_m2 + delta * delta * (n_a * tn / n_ab))
    run_mean[...] = jnp.where(j == 0, blk_mean, run_mean[...] + delta * (tn / n_ab))

    @pl.when(j == n_col - 1)
    def _():
        run_m2[...] = lax.rsqrt(run_m2[...] / d + LN_EPS)

        def row_block(rb, carry):
            t0 = pl.multiple_of(rb * LN_ROWS, LN_ROWS)
            rows = pl.ds(t0, LN_ROWS)
            o_ref[rows, :] = (o_ref[rows, :] - run_mean[rows, :]) * run_m2[rows, :] * g_ref[...] \
                + b_ref[...]
            return carry

        lax.fori_loop(0, tm // LN_ROWS, row_block, 0)


def _out_proj(y_conv, y_gla, y_mem, w_out, x2, ln_g, ln_b, *, tm, tn):
    t, d = x2.shape
    n_col = d // tn
    kdim = w_out.shape[0]
    stat = pltpu.VMEM((tm, 1), F32)
    return pl.pallas_call(
        functools.partial(_out_kernel, tm=tm, tn=tn, n_col=n_col, d=d),
        grid=(t // tm, n_col),
        in_specs=[pl.BlockSpec((tm, y_conv.shape[1]), lambda i, j: (i, 0)),
                  pl.BlockSpec((tm, y_gla.shape[1]), lambda i, j: (i, 0)),
                  pl.BlockSpec((tm, y_mem.shape[1]), lambda i, j: (i, 0)),
                  pl.BlockSpec((kdim, tn), lambda i, j: (0, j)),
                  pl.BlockSpec((tm, tn), lambda i, j: (i, j)),
                  pl.BlockSpec((1, d), lambda i, j: (0, 0)),
                  pl.BlockSpec((1, d), lambda i, j: (0, 0))],
        out_specs=pl.BlockSpec((tm, d), lambda i, j: (i, 0)),
        out_shape=jax.ShapeDtypeStruct((t, d), F32),
        scratch_shapes=[stat, stat],
        compiler_params=pltpu.CompilerParams(
            dimension_semantics=("arbitrary", "arbitrary"), vmem_limit_bytes=VMEM_LIMIT),
        name="out_proj_ln",
    )(y_conv, y_gla, y_mem, w_out, x2, ln_g, ln_b)


_O_QK = 3 * CONV_CH
_O_K = _O_QK + GLA_KEY
_O_V = _O_K + GLA_KEY
_O_LR = _O_V + GLA_VAL
_O_GG = _O_LR + GATE_RANK
_O_MQ = _O_GG + GLA_VAL
_O_MG = _O_MQ + MEM_W


def kernel(x, mem, w_in, conv_dw, conv_dw_b, conv_ln_g, conv_ln_b, gla_w_gate, gla_gate_b,
           gla_norm_g, w_mem_kv, w_out, ln_g, ln_b):
    batch, seq, d = x.shape
    t = batch * seq
    x2 = x.reshape(t, d)
    for l in range(DEPTH):
        w_all = jnp.swapaxes(w_in[l], 0, 1).astype(BF16)
        w_tail = w_all[_O_GG:]
        w_lr = jnp.pad(w_all[_O_LR:_O_GG], ((0, LANES - GATE_RANK), (0, 0)))
        w_gate = jnp.pad(gla_w_gate[l].astype(BF16), ((0, LANES - GATE_RANK), (0, 0)))
        mem_kv = _memkv(mem.reshape(batch * N_MEM, d), w_mem_kv[l])
        mem_kv = mem_kv.reshape(batch, N_MEM, 2 * MEM_W)
        y_mem, x_bf = _mem_group(x2, w_tail, mem_kv, q_col=_O_MQ - _O_GG, gate_col=_O_MG - _O_GG,
                                 batch=batch, seq=seq, tm=512)
        y_conv = _conv_group(x_bf, w_all, conv_dw[l], conv_dw_b[l][None], conv_ln_g[l][None],
                             conv_ln_b[l][None], batch=batch, seq=seq, tm=512)
        y_gla = _gla_group(x_bf, w_all, w_tail, w_lr, w_gate, gla_gate_b[l][None],
                           gla_norm_g[l][None], q_col=_O_QK, k_col=_O_K, v_col=_O_V, gate_col=0,
                           batch=batch, seq=seq, tm=512)
        x2 = _out_proj(y_conv, y_gla, y_mem, w_out[l].astype(BF16), x2, ln_g[l][None],
                       ln_b[l][None], tm=512, tn=1024)
    return x2.reshape(batch, seq, d)
```
